```python
import jax, jax.numpy as jnp
from jax import lax
import numpy as np

D_MODEL = 1024
BATCH = 32
SEQ = 2048
DEPTH = 1
DEC_BATCH = 16
DEC_SEQ = 64
PAST_LEN = 4096

CHUNK = 64
D_MIX = D_MODEL
D_POOL = D_MIX // 2
POOL_WINDOWS = (2, 4, 8, 16)
N_POOL_GROUPS = len(POOL_WINDOWS)
POOL_GROUP_W = D_POOL // N_POOL_GROUPS
POOL_HIST = max(POOL_WINDOWS) - 1
D_FOX = D_MIX - D_POOL
FOX_HEADS = 8
FOX_HEAD_DIM = D_FOX // FOX_HEADS
Q_BLOCK = 128
N_MEM = 256
MEM_HEADS = 4
MEM_HEAD_DIM = D_MODEL // MEM_HEADS
D_FF = ((8 * D_MODEL // 3 + 127) // 128) * 128
CONV_W = 3
D_IN = D_POOL + 3 * D_FOX + FOX_HEADS
FORGET_BIAS = 3.0
EPS = 1e-6

kernel_name = 'hybrid_pool_fox_stream_encoder_step'


def rmsnorm(x, g):
    x32 = x.astype(jnp.float32)
    y = x32 * lax.rsqrt(jnp.mean(x32 * x32, axis=-1, keepdims=True) + EPS)
    return (y * g.astype(jnp.float32)).astype(x.dtype)


def pool_mixer(u, u_hist, pos0, w_pool, pool_scale):
    B, T, _ = u.shape
    nh = u_hist.shape[1]
    u_ext = jnp.concatenate([u_hist.astype(u.dtype), u], axis=1).astype(jnp.float32)
    cs = jnp.cumsum(u_ext, axis=1)
    cs = jnp.concatenate([jnp.zeros((B, 1, D_POOL), jnp.float32), cs], axis=1)
    upper = cs[:, nh + 1:]
    t_idx = jnp.arange(T)
    diffs = []
    for g, w in enumerate(POOL_WINDOWS):
        sl = slice(g * POOL_GROUP_W, (g + 1) * POOL_GROUP_W)
        lo = jnp.maximum(t_idx + nh + 1 - w, 0)
        lower = jnp.take(cs[:, :, sl], lo, axis=1)
        count = jnp.minimum(pos0 + t_idx + 1, w).astype(jnp.float32)
        mean = (upper[:, :, sl] - lower) / count[None, :, None]
        diffs.append(mean - u_ext[:, nh:, sl])
    d = jnp.stack(diffs, axis=2).astype(u.dtype)
    y = jnp.einsum('btgc,gcd->btgd', d, w_pool).reshape(B, T, D_POOL)
    return y * pool_scale


def fox_attention(q, k, v, cq, ck, q_pos, k_pos):
    B, Tq, H, Dh = q.shape
    qb = Q_BLOCK if Tq % Q_BLOCK == 0 else Tq
    nb = Tq // qb
    scale = FOX_HEAD_DIM ** -0.5
    q_blocks = q.reshape(B, nb, qb, H, Dh).transpose(1, 0, 2, 3, 4)
    cq_blocks = cq.reshape(B, nb, qb, H).transpose(1, 0, 3, 2)
    pos_blocks = q_pos.reshape(nb, qb)
    ck_t = ck.transpose(0, 2, 1)

    def one_block(args):
        qblk, cblk, pblk = args
        s = jnp.einsum('bqhd,bkhd->bhqk', qblk, k).astype(jnp.float32) * scale
        s = s + (cblk[:, :, :, None] - ck_t[:, :, None, :])
        causal = pblk[:, None] >= k_pos[None, :]
        s = jnp.where(causal[None, None], s, -jnp.inf)
        p = jax.nn.softmax(s, axis=-1)
        return jnp.einsum('bhqk,bkhd->bqhd', p.astype(v.dtype), v)

    out = lax.map(one_block, (q_blocks, cq_blocks, pos_blocks))
    return out.transpose(1, 0, 2, 3, 4).reshape(B, Tq, H, Dh)


def memory_kv(mem, g_mkv, w_mk, w_mv):
    B, M, _ = mem.shape
    m = rmsnorm(mem, g_mkv)
    mk = (m @ w_mk).reshape(B, M, MEM_HEADS, MEM_HEAD_DIM)
    mv = (m @ w_mv).reshape(B, M, MEM_HEADS, MEM_HEAD_DIM)
    return mk, mv


def encoder_layer(x, k_hist, v_hist, logf_hist, pool_hist, conv_hist, mem_k, mem_v, pos0,
                  g_mix, w_in, b_f, w_pool, pool_scale, w_o, g_xq, w_mq, w_mo,
                  g_ffn, w_up, conv_w, conv_b, w_down):
    B, T, _ = x.shape
    h = rmsnorm(x, g_mix)
    z = h @ w_in
    u = z[..., :D_POOL]
    q = z[..., D_POOL:D_POOL + D_FOX].reshape(B, T, FOX_HEADS, FOX_HEAD_DIM)
    k = z[..., D_POOL + D_FOX:D_POOL + 2 * D_FOX].reshape(B, T, FOX_HEADS, FOX_HEAD_DIM)
    v = z[..., D_POOL + 2 * D_FOX:D_POOL + 3 * D_FOX].reshape(B, T, FOX_HEADS, FOX_HEAD_DIM)
    logf = jax.nn.log_sigmoid((z[..., D_POOL + 3 * D_FOX:] + b_f).astype(jnp.float32))

    nh = k_hist.shape[1]
    k_all = jnp.concatenate([k_hist.astype(k.dtype), k], axis=1)
    v_all = jnp.concatenate([v_hist.astype(v.dtype), v], axis=1)
    c_all = jnp.cumsum(jnp.concatenate([logf_hist.astype(jnp.float32), logf], axis=1), axis=1)
    q_pos = pos0 + jnp.arange(T)
    k_pos = jnp.arange(nh + T) + (pos0 - nh)
    o_fox = fox_attention(q, k_all, v_all, c_all[:, nh:], c_all, q_pos, k_pos).reshape(B, T, D_FOX)

    o_pool = pool_mixer(u, pool_hist, pos0, w_pool, pool_scale)
    x = x + jnp.concatenate([o_pool.astype(x.dtype), o_fox.astype(x.dtype)], axis=-1) @ w_o

    mq = (rmsnorm(x, g_xq) @ w_mq).reshape(B, T, MEM_HEADS, MEM_HEAD_DIM)
    s = jnp.einsum('bthd,bmhd->bhtm', mq, mem_k.astype(mq.dtype)).astype(jnp.float32) * (MEM_HEAD_DIM ** -0.5)
    p = jax.nn.softmax(s, axis=-1)
    mo = jnp.einsum('bhtm,bmhd->bthd', p.astype(x.dtype), mem_v.astype(x.dtype)).reshape(B, T, D_MODEL)
    x = x + mo @ w_mo

    up = rmsnorm(x, g_ffn) @ w_up
    a, b = up[..., :D_FF], up[..., D_FF:]
    a_ext = jnp.concatenate([conv_hist.astype(a.dtype), a], axis=1)
    conv = conv_b + sum(a_ext[:, j:j + T] * conv_w[j] for j in range(CONV_W))
    x = x + (jax.nn.silu(conv) * b) @ w_down

    pool_new = jnp.concatenate([pool_hist.astype(u.dtype), u], axis=1)[:, -POOL_HIST:]
    conv_new = a_ext[:, -(CONV_W - 1):]
    return x, (k, v, logf, pool_new, conv_new)


def setup_inputs(seed: int = 0) -> dict:
    key = jax.random.key(seed)
    ks = jax.random.split(key, 32)
    f32 = jnp.float32
    nrm = lambda k, shape, s: jax.random.normal(k, shape, f32) * s
    inp = {
        'x_prompt': nrm(ks[0], (BATCH, SEQ, D_MODEL), 1.0),
        'x_sample': nrm(ks[1], (DEC_BATCH, DEC_SEQ, D_MODEL), 1.0),
        'cache_fox_k': nrm(ks[2], (DEPTH, DEC_BATCH, PAST_LEN, FOX_HEADS, FOX_HEAD_DIM), 1.0),
        'cache_fox_v': nrm(ks[3], (DEPTH, DEC_BATCH, PAST_LEN, FOX_HEADS, FOX_HEAD_DIM), 1.0),
        'cache_fox_logf': jax.nn.log_sigmoid(FORGET_BIAS + nrm(ks[4], (DEPTH, DEC_BATCH, PAST_LEN, FOX_HEADS), 1.0)),
        'state_pool': nrm(ks[5], (DEPTH, DEC_BATCH, POOL_HIST, D_POOL), 1.0),
        'state_ffn_conv': nrm(ks[6], (DEPTH, DEC_BATCH, CONV_W - 1, D_FF), 1.0),
        'cache_mem_k': nrm(ks[7], (DEPTH, DEC_BATCH, N_MEM, MEM_HEADS, MEM_HEAD_DIM), 1.0),
        'cache_mem_v': nrm(ks[8], (DEPTH, DEC_BATCH, N_MEM, MEM_HEADS, MEM_HEAD_DIM), 1.0),
        'mem_prompt': nrm(ks[9], (BATCH, N_MEM, D_MODEL), 1.0),
        'g_mix': 1.0 + nrm(ks[10], (DEPTH, D_MODEL), 0.05),
        'w_in': nrm(ks[11], (DEPTH, D_MODEL, D_IN), D_MODEL ** -0.5),
        'b_f': FORGET_BIAS + nrm(ks[12], (DEPTH, FOX_HEADS), 0.5),
        'w_pool': nrm(ks[13], (DEPTH, N_POOL_GROUPS, POOL_GROUP_W, POOL_GROUP_W), POOL_GROUP_W ** -0.5),
        'pool_scale': 1.0 + nrm(ks[14], (DEPTH, D_POOL), 0.1),
        'w_o': nrm(ks[15], (DEPTH, D_MIX, D_MODEL), D_MIX ** -0.5),
        'g_xq': 1.0 + nrm(ks[16], (DEPTH, D_MODEL), 0.05),
        'g_mkv': 1.0 + nrm(ks[17], (DEPTH, D_MODEL), 0.05),
        'w_mq': nrm(ks[18], (DEPTH, D_MODEL, D_MODEL), D_MODEL ** -0.5),
        'w_mk': nrm(ks[19], (DEPTH, D_MODEL, D_MODEL), D_MODEL ** -0.5),
        'w_mv': nrm(ks[20], (DEPTH, D_MODEL, D_MODEL), D_MODEL ** -0.5),
        'w_mo': nrm(ks[21], (DEPTH, D_MODEL, D_MODEL), D_MODEL ** -0.5),
        'g_ffn': 1.0 + nrm(ks[22], (DEPTH, D_MODEL), 0.05),
        'w_up': nrm(ks[23], (DEPTH, D_MODEL, 2 * D_FF), D_MODEL ** -0.5),
        'conv_w': nrm(ks[24], (DEPTH, CONV_W, D_FF), CONV_W ** -0.5),
        'conv_b': nrm(ks[25], (DEPTH, D_FF), 0.01),
        'w_down': nrm(ks[26], (DEPTH, D_FF, D_MODEL), D_FF ** -0.5),
        'g_final': 1.0 + nrm(ks[27], (D_MODEL,), 0.05),
    }
    return inp


def reference(x_prompt, x_sample, cache_fox_k, cache_fox_v, cache_fox_logf, state_pool, state_ffn_conv,
              cache_mem_k, cache_mem_v, mem_prompt, g_mix, w_in, b_f, w_pool, pool_scale, w_o,
              g_xq, g_mkv, w_mq, w_mk, w_mv, w_mo, g_ffn, w_up, conv_w, conv_b, w_down, g_final):
    xp, xs = x_prompt, x_sample
    bp = xp.shape[0]
    past = cache_fox_k.shape[2]
    pk, pv, plf, ppool, pconv, pmk, pmv = [], [], [], [], [], [], []
    sk, sv, slf, spool, sconv = [], [], [], [], []
    for l in range(DEPTH):
        lw = (g_mix[l], w_in[l], b_f[l], w_pool[l], pool_scale[l], w_o[l], g_xq[l], w_mq[l], w_mo[l],
              g_ffn[l], w_up[l], conv_w[l], conv_b[l], w_down[l])
        mk, mv = memory_kv(mem_prompt, g_mkv[l], w_mk[l], w_mv[l])
        xp, st_p = encoder_layer(
            xp,
            jnp.zeros((bp, 0, FOX_HEADS, FOX_HEAD_DIM), xp.dtype),
            jnp.zeros((bp, 0, FOX_HEADS, FOX_HEAD_DIM), xp.dtype),
            jnp.zeros((bp, 0, FOX_HEADS), jnp.float32),
            jnp.zeros((bp, 0, D_POOL), xp.dtype),
            jnp.zeros((bp, CONV_W - 1, D_FF), xp.dtype),
            mk, mv, 0, *lw)
        xs, st_s = encoder_layer(
            xs, cache_fox_k[l], cache_fox_v[l], cache_fox_logf[l], state_pool[l], state_ffn_conv[l],
            cache_mem_k[l], cache_mem_v[l], past, *lw)
        pk.append(st_p[0]); pv.append(st_p[1]); plf.append(st_p[2]); ppool.append(st_p[3]); pconv.append(st_p[4])
        pmk.append(mk); pmv.append(mv)
        sk.append(st_s[0]); sv.append(st_s[1]); slf.append(st_s[2]); spool.append(st_s[3]); sconv.append(st_s[4])
    y_prompt = rmsnorm(xp, g_final)
    y_sample = rmsnorm(xs, g_final)
    return (y_prompt, y_sample,
            jnp.stack(pk), jnp.stack(pv), jnp.stack(plf), jnp.stack(ppool), jnp.stack(pconv),
            jnp.stack(pmk), jnp.stack(pmv),
            jnp.stack(sk), jnp.stack(sv), jnp.stack(slf), jnp.stack(spool), jnp.stack(sconv))
```

```python
import functools
import math

import jax
import jax.numpy as jnp
from jax import lax
from jax.experimental import pallas as pl
from jax.experimental.pallas import tpu as pltpu

D_MODEL = 1024
D_POOL = 512
POOL_WINDOWS = (2, 4, 8, 16)
POOL_GROUP_W = 128
POOL_HIST = 15
POOL_PAD = 16
D_FOX = 512
FOX_HEADS = 8
FOX_HEAD_DIM = 64
N_MEM = 256
MEM_HEADS = 4
MEM_HEAD_DIM = 256
D_FF = 2816
CONV_W = 3
CONV_PAD = 8
FF_CHUNK = 1408
EPS = 1e-6
LANES = 128
VMEM_LIMIT = 60 * 1024 * 1024

F32 = jnp.float32
BF16 = jnp.bfloat16


def _dot(a, b):
    return jnp.dot(a, b, preferred_element_type=F32)


def _dot_nt(a, b):
    return lax.dot_general(a, b, (((1,), (1,)), ((), ())), preferred_element_type=F32)


def _rms(x, g):
    ms = jnp.mean(x * x, axis=-1, keepdims=True)
    return (x * lax.rsqrt(ms + EPS)) * g


def _const_spec(shape):
    nd = len(shape)
    return pl.BlockSpec(shape, lambda *_: (0,) * nd, pipeline_mode=pl.Buffered(1))


def _params(sem):
    return pltpu.CompilerParams(dimension_semantics=sem, vmem_limit_bytes=VMEM_LIMIT)


def _inproj_kernel(x_ref, g_ref, w_ref, wf_ref, bf_ref, u_ref, q_ref, k_ref, v_ref, lf_ref):
    h = _rms(x_ref[...], g_ref[...]).astype(BF16)
    u_ref[...] = _dot(h, w_ref[:, 0:D_POOL])
    q_ref[...] = (_dot(h, w_ref[:, D_POOL:D_POOL + D_FOX]) * (FOX_HEAD_DIM ** -0.5)).astype(BF16)
    k_ref[...] = _dot(h, w_ref[:, D_POOL + D_FOX:D_POOL + 2 * D_FOX])
    v_ref[...] = _dot(h, w_ref[:, D_POOL + 2 * D_FOX:D_POOL + 3 * D_FOX])
    zf = _dot(h, wf_ref[...]) + bf_ref[...]
    lf = -(jnp.maximum(-zf, 0.0) + jnp.log1p(jnp.exp(-jnp.abs(zf))))
    lf_ref[...] = lf[:, 0:FOX_HEADS]


def _inproj(x2d, g, w_main, w_f, b_f, tm):
    n = x2d.shape[0]
    row = lambda i: (i, 0)
    return pl.pallas_call(
        _inproj_kernel,
        grid=(n // tm,),
        in_specs=[
            pl.BlockSpec((tm, D_MODEL), row),
            _const_spec((1, D_MODEL)),
            _const_spec((D_MODEL, D_POOL + 3 * D_FOX)),
            _const_spec((D_MODEL, LANES)),
            _const_spec((1, LANES)),
        ],
        out_specs=[
            pl.BlockSpec((tm, D_POOL), row),
            pl.BlockSpec((tm, D_FOX), row),
            pl.BlockSpec((tm, D_FOX), row),
            pl.BlockSpec((tm, D_FOX), row),
            pl.BlockSpec((tm, FOX_HEADS), row),
        ],
        out_shape=[
            jax.ShapeDtypeStruct((n, D_POOL), F32),
            jax.ShapeDtypeStruct((n, D_FOX), BF16),
            jax.ShapeDtypeStruct((n, D_FOX), F32),
            jax.ShapeDtypeStruct((n, D_FOX), F32),
            jax.ShapeDtypeStruct((n, FOX_HEADS), F32),
        ],
        compiler_params=_params(("arbitrary",)),
        name="inproj",
    )(x2d, g, w_main, w_f, b_f)


def _memkv_kernel(m_ref, g_ref, wk_ref, wv_ref, mk_ref, mv_ref):
    h = _rms(m_ref[...], g_ref[...]).astype(BF16)
    mk_ref[...] = _dot(h, wk_ref[...])
    mv_ref[...] = _dot(h, wv_ref[...])


def _memkv(m2d, g, wk, wv, tm):
    n = m2d.shape[0]
    row = lambda i: (i, 0)
    return pl.pallas_call(
        _memkv_kernel,
        grid=(n // tm,),
        in_specs=[
            pl.BlockSpec((tm, D_MODEL), row),
            _const_spec((1, D_MODEL)),
            _const_spec((D_MODEL, D_MODEL)),
            _const_spec((D_MODEL, D_MODEL)),
        ],
        out_specs=[pl.BlockSpec((tm, D_MODEL), row), pl.BlockSpec((tm, D_MODEL), row)],
        out_shape=[jax.ShapeDtypeStruct((n, D_MODEL), F32)] * 2,
        compiler_params=_params(("arbitrary",)),
        name="memkv",
    )(m2d, g, wk, wv)


def _tril(n):
    r = lax.broadcasted_iota(jnp.int32, (n, n), 0)
    c = lax.broadcasted_iota(jnp.int32, (n, n), 1)
    return jnp.where(r >= c, 1.0, 0.0).astype(BF16)


def _cumsum_rows(x, tril, carry):
    hi = x.astype(BF16)
    r1 = x - hi.astype(F32)
    mid = r1.astype(BF16)
    lo = (r1 - mid.astype(F32)).astype(BF16)
    return _dot(tril, hi) + _dot(tril, mid) + _dot(tril, lo) + carry


def _pool_mixer(uext_ref, tq, t0, pos0, wpool_ref, pscale_ref, osc_ref):
    t_idx = t0 + lax.broadcasted_iota(jnp.int32, (tq, 1), 0)
    for g, w in enumerate(POOL_WINDOWS):
        sl = slice(g * POOL_GROUP_W, (g + 1) * POOL_GROUP_W)
        u = uext_ref[POOL_PAD:POOL_PAD + tq, sl]
        s = u
        for d in range(1, w):
            s = s + uext_ref[POOL_PAD - d:POOL_PAD - d + tq, sl]
        count = jnp.minimum(pos0 + t_idx + 1, w).astype(F32)
        diff = (s / count - u).astype(BF16)
        y = _dot(diff, wpool_ref[g]) * pscale_ref[:, sl]
        osc_ref[:, sl] = y.astype(BF16)


def _mix_prompt_kernel(x_ref, u_ref, q_ref, k_ref, v_ref, lf_ref, wpool_ref, pscale_ref, wo_ref,
                       x1_ref, ksc, vsc, ccol, crow, uext, osc, *, nq, tq):
    i = pl.program_id(1)

    @pl.when(i == 0)
    def _stage():
        for h in range(FOX_HEADS):
            hs = slice(h * FOX_HEAD_DIM, (h + 1) * FOX_HEAD_DIM)
            for b in range(nq):
                rs = slice(b * tq, (b + 1) * tq)
                ksc[h * nq + b] = k_ref[0, rs, hs].astype(BF16)
                vsc[h * nq + b] = v_ref[0, rs, hs].astype(BF16)
        tril = _tril(tq)
        ccol[...] = jnp.zeros_like(ccol)
        carry = jnp.zeros((1, LANES), F32)
        for b in range(nq):
            ccol[b, :, 0:FOX_HEADS] = lf_ref[0, b * tq:(b + 1) * tq, :]
            c = _cumsum_rows(ccol[b], tril, carry)
            carry = c[tq - 1:tq, :]
            ccol[b] = c
            crow[b] = c.T[0:FOX_HEADS, :]
        uext[0:POOL_PAD, :] = jnp.zeros((POOL_PAD, D_POOL), F32)

    uext[POOL_PAD:POOL_PAD + tq, :] = u_ref[0]
    _pool_mixer(uext, tq, i * tq, 0, wpool_ref, pscale_ref, osc)
    uext[0:POOL_PAD, :] = uext[tq:tq + POOL_PAD, :]

    r_idx = lax.broadcasted_iota(jnp.int32, (tq, tq), 0)
    c_idx = lax.broadcasted_iota(jnp.int32, (tq, tq), 1)
    causal = r_idx >= c_idx
    for h in range(FOX_HEADS):
        hs = slice(h * FOX_HEAD_DIM, (h + 1) * FOX_HEAD_DIM)
        q_h = q_ref[0, :, hs]
        cq = ccol[i, :, h:h + 1]

        def scores(j):
            s = _dot_nt(q_h, ksc[h * nq + j])
            return s + (cq - crow[j, h:h + 1, :])

        def update(j, s, carry):
            m, l, acc = carry
            m_new = jnp.maximum(m, jnp.max(s, axis=-1, keepdims=True))
            alpha = jnp.exp(m - m_new)
            p = jnp.exp(s - m_new)
            l = alpha * l + jnp.sum(p, axis=-1, keepdims=True)
            acc = alpha * acc + _dot(p.astype(BF16), vsc[h * nq + j])
            return m_new, l, acc

        def body(j, carry):
            return update(j, scores(j), carry)

        init = (jnp.full((tq, 1), -jnp.inf, F32), jnp.zeros((tq, 1), F32),
                jnp.zeros((tq, FOX_HEAD_DIM), F32))
        carry = lax.fori_loop(0, i, body, init)
        s_diag = jnp.where(causal, scores(i), -jnp.inf)
        _, l, acc = update(i, s_diag, carry)
        osc[:, D_POOL + h * FOX_HEAD_DIM:D_POOL + (h + 1) * FOX_HEAD_DIM] = (acc / l).astype(BF16)

    x1_ref[0] = x_ref[0] + _dot(osc[...], wo_ref[...])


def _mix_prompt(x, u, q, k, v, lf, w_pool, pool_scale, w_o, tq):
    bsz, t, _ = x.shape
    nq = t // tq
    blk = lambda b, i: (b, i, 0)
    full = lambda b, i: (b, 0, 0)
    kern = functools.partial(_mix_prompt_kernel, nq=nq, tq=tq)
    return pl.pallas_call(
        kern,
        grid=(bsz, nq),
        in_specs=[
            pl.BlockSpec((1, tq, D_MODEL), blk),
            pl.BlockSpec((1, tq, D_POOL), blk),
            pl.BlockSpec((1, tq, D_FOX), blk),
            pl.BlockSpec((1, t, D_FOX), full),
            pl.BlockSpec((1, t, D_FOX), full),
            pl.BlockSpec((1, t, FOX_HEADS), full),
            _const_spec((len(POOL_WINDOWS), POOL_GROUP_W, POOL_GROUP_W)),
            _const_spec((1, D_POOL)),
            _const_spec((D_MODEL, D_MODEL)),
        ],
        out_specs=pl.BlockSpec((1, tq, D_MODEL), blk),
        out_shape=jax.ShapeDtypeStruct((bsz, t, D_MODEL), F32),
        scratch_shapes=[
            pltpu.VMEM((FOX_HEADS * nq, tq, FOX_HEAD_DIM), BF16),
            pltpu.VMEM((FOX_HEADS * nq, tq, FOX_HEAD_DIM), BF16),
            pltpu.VMEM((nq, tq, LANES), F32),
            pltpu.VMEM((nq, FOX_HEADS, tq), F32),
            pltpu.VMEM((POOL_PAD + tq, D_POOL), F32),
            pltpu.VMEM((tq, D_MODEL), BF16),
        ],
        compiler_params=_params(("arbitrary", "arbitrary")),
        name="mix_prompt",
    )(x, u, q, k, v, lf, w_pool, pool_scale, w_o)


def _mix_sample_kernel(x_ref, u_ref, q_ref, k_ref, v_ref, lf_ref, kc_ref, vc_ref, lfc_ref, ph_ref,
                       wpool_ref, pscale_ref, wo_ref, x1_ref, cbuf, crow, uext, osc, *, past, t, tc):
    nblk = past // tc
    tril = _tril(tc)
    cbuf[...] = jnp.zeros_like(cbuf)
    carry = jnp.zeros((1, LANES), F32)
    for b in range(nblk):
        cbuf[:, 0:FOX_HEADS] = lfc_ref[0, b * tc:(b + 1) * tc, :]
        c = _cumsum_rows(cbuf[...], tril, carry)
        carry = c[tc - 1:tc, :]
        crow[:, b * tc:(b + 1) * tc] = c.T[0:FOX_HEADS, :]
    cbuf[...] = jnp.zeros_like(cbuf)
    cbuf[0:t, 0:FOX_HEADS] = lf_ref[0]
    cnew = _cumsum_rows(cbuf[...], tril, carry)
    cnew_row = cnew.T[0:FOX_HEADS, 0:t]

    uext[0:1, :] = jnp.zeros((1, D_POOL), F32)
    uext[1:POOL_PAD, :] = ph_ref[0]
    uext[POOL_PAD:POOL_PAD + t, :] = u_ref[0]
    _pool_mixer(uext, t, 0, past, wpool_ref, pscale_ref, osc)

    r_idx = lax.broadcasted_iota(jnp.int32, (t, t), 0)
    c_idx = lax.broadcasted_iota(jnp.int32, (t, t), 1)
    causal = r_idx >= c_idx
    for h in range(FOX_HEADS):
        hs = slice(h * FOX_HEAD_DIM, (h + 1) * FOX_HEAD_DIM)
        q_h = q_ref[0, :, hs]
        cq = cnew[0:t, h:h + 1]
        s_hist = _dot_nt(q_h, kc_ref[0, :, hs].astype(BF16)) + (cq - crow[h:h + 1, :])
        s_new = _dot_nt(q_h, k_ref[0, :, hs].astype(BF16)) + (cq - cnew_row[h:h + 1, :])
        s_new = jnp.where(causal, s_new, -jnp.inf)
        m = jnp.maximum(jnp.max(s_hist, axis=-1, keepdims=True), jnp.max(s_new, axis=-1, keepdims=True))
        p_hist = jnp.exp(s_hist - m)
        p_new = jnp.exp(s_new - m)
        l = jnp.sum(p_hist, axis=-1, keepdims=True) + jnp.sum(p_new, axis=-1, keepdims=True)
        acc = (_dot(p_hist.astype(BF16), vc_ref[0, :, hs].astype(BF16))
               + _dot(p_new.astype(BF16), v_ref[0, :, hs].astype(BF16)))
        osc[:, D_POOL + h * FOX_HEAD_DIM:D_POOL + (h + 1) * FOX_HEAD_DIM] = (acc / l).astype(BF16)

    x1_ref[0] = x_ref[0] + _dot(osc[...], wo_ref[...])


def _mix_sample(x, u, q, k, v, lf, kc, vc, lfc, ph, w_pool, pool_scale, w_o, tc):
    bsz, t, _ = x.shape
    past = kc.shape[1]
    assert t <= tc and past % tc == 0 and t >= POOL_HIST
    one = lambda b: (b, 0, 0)
    kern = functools.partial(_mix_sample_kernel, past=past, t=t, tc=tc)
    return pl.pallas_call(
        kern,
        grid=(bsz,),
        in_specs=[
            pl.BlockSpec((1, t, D_MODEL), one),
            pl.BlockSpec((1, t, D_POOL), one),
            pl.BlockSpec((1, t, D_FOX), one),
            pl.BlockSpec((1, t, D_FOX), one),
            pl.BlockSpec((1, t, D_FOX), one),
            pl.BlockSpec((1, t, FOX_HEADS), one),
            pl.BlockSpec((1, past, D_FOX), one),
            pl.BlockSpec((1, past, D_FOX), one),
            pl.BlockSpec((1, past, FOX_HEADS), one),
            pl.BlockSpec((1, POOL_HIST, D_POOL), one),
            _const_spec((len(POOL_WINDOWS), POOL_GROUP_W, POOL_GROUP_W)),
            _const_spec((1, D_POOL)),
            _const_spec((D_MODEL, D_MODEL)),
        ],
        out_specs=pl.BlockSpec((1, t, D_MODEL), one),
        out_shape=jax.ShapeDtypeStruct((bsz, t, D_MODEL), F32),
        scratch_shapes=[
            pltpu.VMEM((tc, LANES), F32),
            pltpu.VMEM((FOX_HEADS, past), F32),
            pltpu.VMEM((POOL_PAD + t, D_POOL), F32),
            pltpu.VMEM((t, D_MODEL), BF16),
        ],
        compiler_params=_params(("arbitrary",)),
        name="mix_sample",
    )(x, u, q, k, v, lf, kc, vc, lfc, ph, w_pool, pool_scale, w_o)


def _memffn_kernel(x_ref, mk_ref, mv_ref, ch_ref, gxq_ref, wmq_ref, wmo_ref, gffn_ref, wup_ref,
                   cw_ref, cb_ref, wdown_ref, gfin_ref, y_ref, cs_ref, aext, mo_sc, *, tm, final_norm):
    i = pl.program_id(1)
    x = x_ref[0]

    hq = _rms(x, gxq_ref[...]).astype(BF16)
    mq = (_dot(hq, wmq_ref[...]) * (MEM_HEAD_DIM ** -0.5)).astype(BF16)
    for h in range(MEM_HEADS):
        hs = slice(h * MEM_HEAD_DIM, (h + 1) * MEM_HEAD_DIM)
        s = _dot_nt(mq[:, hs], mk_ref[0, :, hs].astype(BF16))
        m = jnp.max(s, axis=-1, keepdims=True)
        p = jnp.exp(s - m)
        l = jnp.sum(p, axis=-1, keepdims=True)
        o = _dot(p.astype(BF16), mv_ref[0, :, hs].astype(BF16)) / l
        mo_sc[:, hs] = o.astype(BF16)
    x2 = x + _dot(mo_sc[...], wmo_ref[...])

    @pl.when(i == 0)
    def _hist():
        aext[CONV_PAD - (CONV_W - 1):CONV_PAD, :] = ch_ref[0]

    hf = _rms(x2, gffn_ref[...]).astype(BF16)
    acc = jnp.zeros((tm, D_MODEL), F32)
    for c in range(D_FF // FF_CHUNK):
        cs = slice(c * FF_CHUNK, (c + 1) * FF_CHUNK)
        a = _dot(hf, wup_ref[:, cs])
        b = _dot(hf, wup_ref[:, D_FF + c * FF_CHUNK:D_FF + (c + 1) * FF_CHUNK])
        aext[CONV_PAD:CONV_PAD + tm, cs] = a
        conv = (aext[CONV_PAD - 2:CONV_PAD - 2 + tm, cs] * cw_ref[0:1, cs]
                + aext[CONV_PAD - 1:CONV_PAD - 1 + tm, cs] * cw_ref[1:2, cs]
                + a * cw_ref[2:3, cs])
        conv = cb_ref[:, cs] + conv
        gate = conv * (1.0 / (1.0 + jnp.exp(-conv)))
        acc = acc + _dot((gate * b).astype(BF16), wdown_ref[cs, :])
    cs_ref[0] = aext[CONV_PAD + tm - (CONV_W - 1):CONV_PAD + tm, :]
    aext[0:CONV_PAD, :] = aext[tm:tm + CONV_PAD, :]
    x3 = x2 + acc
    if final_norm:
        x3 = _rms(x3, gfin_ref[...])
    y_ref[0] = x3


def _memffn(x, mk, mv, conv_hist, g_xq, w_mq, w_mo, g_ffn, w_up, conv_w, conv_b, w_down, g_final,
            tm, final_norm):
    bsz, t, _ = x.shape
    blk = lambda b, i: (b, i, 0)
    full = lambda b, i: (b, 0, 0)
    kern = functools.partial(_memffn_kernel, tm=tm, final_norm=final_norm)
    return pl.pallas_call(
        kern,
        grid=(bsz, t // tm),
        in_specs=[
            pl.BlockSpec((1, tm, D_MODEL), blk),
            pl.BlockSpec((1, N_MEM, D_MODEL), full),
            pl.BlockSpec((1, N_MEM, D_MODEL), full),
            pl.BlockSpec((1, CONV_W - 1, D_FF), full),
            _const_spec((1, D_MODEL)),
            _const_spec((D_MODEL, D_MODEL)),
            _const_spec((D_MODEL, D_MODEL)),
            _const_spec((1, D_MODEL)),
            _const_spec((D_MODEL, 2 * D_FF)),
            _const_spec((CONV_W, D_FF)),
            _const_spec((1, D_FF)),
            _const_spec((D_FF, D_MODEL)),
            _const_spec((1, D_MODEL)),
        ],
        out_specs=[
            pl.BlockSpec((1, tm, D_MODEL), blk),
            pl.BlockSpec((1, CONV_W - 1, D_FF), full),
        ],
        out_shape=[
            jax.ShapeDtypeStruct((bsz, t, D_MODEL), F32),
            jax.ShapeDtypeStruct((bsz, CONV_W - 1, D_FF), F32),
        ],
        scratch_shapes=[
            pltpu.VMEM((CONV_PAD + tm, D_FF), F32),
            pltpu.VMEM((tm, D_MODEL), BF16),
        ],
        compiler_params=_params(("arbitrary", "arbitrary")),
        name="memffn",
    )(x, mk, mv, conv_hist, g_xq, w_mq, w_mo, g_ffn, w_up, conv_w, conv_b, w_down, g_final)


def _row_tile(n, target):
    tm = min(n, target)
    assert n % tm == 0
    return tm


def kernel(x_prompt, x_sample, cache_fox_k, cache_fox_v, cache_fox_logf, state_pool, state_ffn_conv, cache_mem_k, cache_mem_v, mem_prompt, g_mix, w_in, b_f, w_pool, pool_scale, w_o, g_xq, g_mkv, w_mq, w_mk, w_mv, w_mo, g_ffn, w_up, conv_w, conv_b, w_down, g_final):
    depth = w_in.shape[0]
    bp, tp, _ = x_prompt.shape
    bs, ts, _ = x_sample.shape
    past = cache_fox_k.shape[2]
    n_main = D_POOL + 3 * D_FOX

    xp, xs = x_prompt, x_sample
    outs = {name: [] for name in ("pk", "pv", "plf", "ppool", "pconv", "pmk", "pmv",
                                   "sk", "sv", "slf", "spool", "sconv")}
    for l in range(depth):
        last = l == depth - 1
        row2 = lambda a: a.reshape(1, -1).astype(F32)
        w_main = w_in[l][:, :n_main].astype(BF16)
        w_f = jnp.pad(w_in[l][:, n_main:], ((0, 0), (0, LANES - FOX_HEADS))).astype(BF16)
        bf = jnp.pad(b_f[l].astype(F32), (0, LANES - FOX_HEADS)).reshape(1, LANES)
        wp = w_pool[l].astype(BF16)
        ps = row2(pool_scale[l])
        wo = w_o[l].astype(BF16)
        ffn_w = (row2(g_xq[l]), w_mq[l].astype(BF16), w_mo[l].astype(BF16), row2(g_ffn[l]),
                 w_up[l].astype(BF16), conv_w[l].astype(F32), row2(conv_b[l]), w_down[l].astype(BF16),
                 row2(g_final))

        mk, mv = _memkv(mem_prompt.reshape(bp * N_MEM, D_MODEL), row2(g_mkv[l]),
                        w_mk[l].astype(BF16), w_mv[l].astype(BF16), _row_tile(bp * N_MEM, 512))
        mk = mk.reshape(bp, N_MEM, D_MODEL)
        mv = mv.reshape(bp, N_MEM, D_MODEL)
        u, q, k, v, lf = _inproj(xp.reshape(bp * tp, D_MODEL), row2(g_mix[l]), w_main, w_f, bf,
                                 _row_tile(bp * tp, 512))
        u = u.reshape(bp, tp, D_POOL)
        q = q.reshape(bp, tp, D_FOX)
        k = k.reshape(bp, tp, D_FOX)
        v = v.reshape(bp, tp, D_FOX)
        lf = lf.reshape(bp, tp, FOX_HEADS)
        x1 = _mix_prompt(xp, u, q, k, v, lf, wp, ps, wo, _row_tile(tp, 256))
        xp, pconv = _memffn(x1, mk, mv, jnp.zeros((bp, CONV_W - 1, D_FF), F32), *ffn_w,
                            tm=_row_tile(tp, 256), final_norm=last)
        outs["pk"].append(k.reshape(bp, tp, FOX_HEADS, FOX_HEAD_DIM))
        outs["pv"].append(v.reshape(bp, tp, FOX_HEADS, FOX_HEAD_DIM))
        outs["plf"].append(lf)
        outs["ppool"].append(u[:, tp - POOL_HIST:, :])
        outs["pconv"].append(pconv)
        outs["pmk"].append(mk.reshape(bp, N_MEM, MEM_HEADS, MEM_HEAD_DIM))
        outs["pmv"].append(mv.reshape(bp, N_MEM, MEM_HEADS, MEM_HEAD_DIM))

        u, q, k, v, lf = _inproj(xs.reshape(bs * ts, D_MODEL), row2(g_mix[l]), w_main, w_f, bf,
                                 _row_tile(bs * ts, 512))
        u = u.reshape(bs, ts, D_POOL)
        q = q.reshape(bs, ts, D_FOX)
        k = k.reshape(bs, ts, D_FOX)
        v = v.reshape(bs, ts, D_FOX)
        lf = lf.reshape(bs, ts, FOX_HEADS)
        x1 = _mix_sample(xs, u, q, k, v, lf,
                         cache_fox_k[l].reshape(bs, past, D_FOX), cache_fox_v[l].reshape(bs, past, D_FOX),
                         cache_fox_logf[l].astype(F32), state_pool[l], wp, ps, wo, tc=512)
        xs, sconv = _memffn(x1, cache_mem_k[l].reshape(bs, N_MEM, D_MODEL),
                            cache_mem_v[l].reshape(bs, N_MEM, D_MODEL), state_ffn_conv[l], *ffn_w,
                            tm=ts, final_norm=last)
        outs["sk"].append(k.reshape(bs, ts, FOX_HEADS, FOX_HEAD_DIM))
        outs["sv"].append(v.reshape(bs, ts, FOX_HEADS, FOX_HEAD_DIM))
        outs["slf"].append(lf)
        outs["spool"].append(u[:, ts - POOL_HIST:, :])
        outs["sconv"].append(sconv)

    st = {name: jnp.stack(vals) for name, vals in outs.items()}
    return (xp, xs, st["pk"], st["pv"], st["plf"], st["ppool"], st["pconv"], st["pmk"], st["pmv"],
            st["sk"], st["sv"], st["slf"], st["spool"], st["sconv"])
```

```python
import functools
import math

import jax
import jax.numpy as jnp
from jax import lax
from jax.experimental import pallas as pl
from jax.experimental.pallas import tpu as pltpu

D_MODEL = 1024
D_POOL = 512
POOL_WINDOWS = (2, 4, 8, 16)
POOL_GROUP_W = 128
POOL_HIST = 15
POOL_CARRY = 16
POOL_BASE = 24
D_FOX = 512
FOX_HEADS = 8
FOX_HEAD_DIM = 64
FOX_PAIRS = FOX_HEADS // 2
N_MEM = 256
MEM_HEADS = 4
MEM_HEAD_DIM = 256
D_FF = 2816
CONV_W = 3
CONV_PAD = 8
FF_CHUNKS = ((0, 1024), (1024, 2048), (2048, 2816))
EPS = 1e-6
LOG2E = math.log2(math.e)
LANES = 128
VMEM_LIMIT = 60 * 1024 * 1024

F32 = jnp.float32
BF16 = jnp.bfloat16


def _dot(a, b):
    return jnp.dot(a, b, preferred_element_type=F32)


def _dot_nt(a, b):
    return lax.dot_general(a, b, (((1,), (1,)), ((), ())), preferred_element_type=F32)


def _rms(x, g):
    ms = jnp.mean(x * x, axis=-1, keepdims=True)
    return (x * lax.rsqrt(ms + EPS)) * g


def _const_spec(shape):
    nd = len(shape)
    return pl.BlockSpec(shape, lambda *_: (0,) * nd, pipeline_mode=pl.Buffered(1))


def _params(sem):
    return pltpu.CompilerParams(dimension_semantics=sem, vmem_limit_bytes=VMEM_LIMIT)


def _tril(n):
    r = lax.broadcasted_iota(jnp.int32, (n, n), 0)
    c = lax.broadcasted_iota(jnp.int32, (n, n), 1)
    return jnp.where(r >= c, 1.0, 0.0).astype(BF16)


def _triu(n):
    r = lax.broadcasted_iota(jnp.int32, (n, n), 0)
    c = lax.broadcasted_iota(jnp.int32, (n, n), 1)
    return jnp.where(r <= c, 1.0, 0.0).astype(BF16)


def _split3(x):
    hi = x.astype(BF16)
    r1 = x - hi.astype(F32)
    mid = r1.astype(BF16)
    lo = (r1 - mid.astype(F32)).astype(BF16)
    return hi, mid, lo


def _cumsum_rows(x, tril, carry):
    hi, mid, lo = _split3(x)
    return _dot(tril, hi) + _dot(tril, mid) + _dot(tril, lo) + carry


def _cumsum_lanes(x, triu, carry):
    hi, mid, lo = _split3(x)
    return _dot(hi, triu) + _dot(mid, triu) + _dot(lo, triu) + carry


def _pool_mixer(uext_ref, tmp_a, tmp_b, tq, t0, pos0, wpool_ref, pscale_ref, out_ref):
    lo, hi = 8, POOL_BASE + tq
    t_idx = t0 + lax.broadcasted_iota(jnp.int32, (tq, 1), 0)
    for g, w in enumerate(POOL_WINDOWS):
        sl = slice(g * POOL_GROUP_W, (g + 1) * POOL_GROUP_W)
        src, bufs, d = uext_ref, (tmp_a, tmp_b), 1
        col = sl
        while 2 * d < w:
            dst = bufs[0]
            dst[lo:hi, :] = src[lo:hi, col] + src[lo - d:hi - d, col]
            src, bufs, d, col = dst, (bufs[1], bufs[0]), 2 * d, slice(0, POOL_GROUP_W)
        s = src[POOL_BASE:hi, col] + src[POOL_BASE - d:hi - d, col]
        u = uext_ref[POOL_BASE:hi, sl]
        count = jnp.minimum(pos0 + t_idx + 1, w).astype(F32)
        diff = (s / count - u).astype(BF16)
        y = _dot(diff, wpool_ref[g]) * pscale_ref[:, sl]
        out_ref[:, sl] = y.astype(out_ref.dtype)


def _softmax_stats_p(s_list, cq):
    m = s_list[0]
    for s in s_list[1:]:
        m = jnp.maximum(m, s)
    m_row = jnp.max(m, axis=-1, keepdims=True) + cq
    r = cq - m_row
    p_list = [jnp.exp2(s + r) for s in s_list]
    l = p_list[0]
    for p in p_list[1:]:
        l = l + p
    return p_list, jnp.sum(l, axis=-1, keepdims=True)


def _inproj_kernel(x_ref, g_ref, w_ref, wf_ref, bf_ref, wpool_ref, pscale_ref, *refs, tm, bps, pool):
    if pool:
        op_ref, ut_ref, q_ref, k_ref, v_ref, lf_ref, uext, tmp_a, tmp_b = refs
    else:
        u_ref, q_ref, k_ref, v_ref, lf_ref = refs
    h = _rms(x_ref[...], g_ref[...]).astype(BF16)
    u = _dot(h, w_ref[:, 0:D_POOL])
    q_ref[...] = (_dot(h, w_ref[:, D_POOL:D_POOL + D_FOX]) * (FOX_HEAD_DIM ** -0.5 * LOG2E)).astype(BF16)
    k = _dot(h, w_ref[:, D_POOL + D_FOX:D_POOL + 2 * D_FOX])
    v = _dot(h, w_ref[:, D_POOL + 2 * D_FOX:D_POOL + 3 * D_FOX])
    zf = _dot(h, wf_ref[...]) + bf_ref[...]
    lf = -(jnp.maximum(-zf, 0.0) + jnp.log1p(jnp.exp(-jnp.abs(zf))))
    if pool:
        k_ref[0] = k.T.reshape(FOX_HEADS, FOX_HEAD_DIM, tm)
        v_ref[0] = v.T.reshape(FOX_HEADS, FOX_HEAD_DIM, tm)
        lf_ref[0] = lf.T[0:FOX_HEADS, :]
    else:
        k_ref[...] = k
        v_ref[...] = v
        lf_ref[...] = lf[:, 0:FOX_HEADS]
    if pool:
        i = pl.program_id(0)
        blk = i % bps

        @pl.when(blk == 0)
        def _no_history():
            uext[0:POOL_BASE, :] = jnp.zeros((POOL_BASE, D_POOL), F32)
            tmp_a[0:8, :] = jnp.zeros((8, POOL_GROUP_W), F32)
            tmp_b[0:8, :] = jnp.zeros((8, POOL_GROUP_W), F32)

        uext[POOL_BASE:POOL_BASE + tm, :] = u
        _pool_mixer(uext, tmp_a, tmp_b, tm, blk * tm, 0, wpool_ref, pscale_ref, op_ref)
        tail = uext[tm + POOL_BASE - POOL_CARRY:tm + POOL_BASE, :]
        ut_ref[0] = tail
        uext[POOL_BASE - POOL_CARRY:POOL_BASE, :] = tail
    else:
        u_ref[...] = u


def _inproj(x2d, g, w_main, w_f, b_f, w_pool, pool_scale, tm, seq_len, pool):
    n = x2d.shape[0]
    bps = seq_len // tm
    row = lambda i: (i, 0)
    if pool:
        nb = n // seq_len
        kv_spec = pl.BlockSpec((1, FOX_HEADS, FOX_HEAD_DIM, tm), lambda i: (i // bps, 0, 0, i % bps))
        kv_shape = jax.ShapeDtypeStruct((nb, FOX_HEADS, FOX_HEAD_DIM, seq_len), F32)
        out_specs = [pl.BlockSpec((tm, D_POOL), row),
                     pl.BlockSpec((1, POOL_CARRY, D_POOL), lambda i: (i // bps, 0, 0)),
                     pl.BlockSpec((tm, D_FOX), row), kv_spec, kv_spec,
                     pl.BlockSpec((1, FOX_HEADS, tm), lambda i: (i // bps, 0, i % bps))]
        out_shape = [jax.ShapeDtypeStruct((n, D_POOL), BF16),
                     jax.ShapeDtypeStruct((nb, POOL_CARRY, D_POOL), F32),
                     jax.ShapeDtypeStruct((n, D_FOX), BF16), kv_shape, kv_shape,
                     jax.ShapeDtypeStruct((nb, FOX_HEADS, seq_len), F32)]
        scratch = [pltpu.VMEM((POOL_BASE + tm, D_POOL), F32),
                   pltpu.VMEM((POOL_BASE + tm, POOL_GROUP_W), F32),
                   pltpu.VMEM((POOL_BASE + tm, POOL_GROUP_W), F32)]
    else:
        out_specs = [pl.BlockSpec((tm, D_POOL), row), pl.BlockSpec((tm, D_FOX), row),
                     pl.BlockSpec((tm, D_FOX), row), pl.BlockSpec((tm, D_FOX), row),
                     pl.BlockSpec((tm, FOX_HEADS), row)]
        out_shape = [jax.ShapeDtypeStruct((n, D_POOL), F32), jax.ShapeDtypeStruct((n, D_FOX), BF16),
                     jax.ShapeDtypeStruct((n, D_FOX), F32), jax.ShapeDtypeStruct((n, D_FOX), F32),
                     jax.ShapeDtypeStruct((n, FOX_HEADS), F32)]
        scratch = []
    return pl.pallas_call(
        functools.partial(_inproj_kernel, tm=tm, bps=bps, pool=pool),
        grid=(n // tm,),
        in_specs=[
            pl.BlockSpec((tm, D_MODEL), row),
            _const_spec((1, D_MODEL)),
            _const_spec((D_MODEL, D_POOL + 3 * D_FOX)),
            _const_spec((D_MODEL, LANES)),
            _const_spec((1, LANES)),
            _const_spec((len(POOL_WINDOWS), POOL_GROUP_W, POOL_GROUP_W)),
            _const_spec((1, D_POOL)),
        ],
        out_specs=out_specs,
        out_shape=out_shape,
        scratch_shapes=scratch,
        compiler_params=_params(("arbitrary",)),
        name="inproj_pool" if pool else "inproj",
    )(x2d, g, w_main, w_f, b_f, w_pool, pool_scale)


def _memkv_kernel(m_ref, g_ref, wk_ref, wv_ref, mk_ref, mv_ref):
    h = _rms(m_ref[...], g_ref[...]).astype(BF16)
    mk_ref[...] = _dot(h, wk_ref[...])
    mv_ref[...] = _dot(h, wv_ref[...])


def _memkv(m2d, g, wk, wv, tm):
    n = m2d.shape[0]
    row = lambda i: (i, 0)
    return pl.pallas_call(
        _memkv_kernel,
        grid=(n // tm,),
        in_specs=[
            pl.BlockSpec((tm, D_MODEL), row),
            _const_spec((1, D_MODEL)),
            _const_spec((D_MODEL, D_MODEL)),
            _const_spec((D_MODEL, D_MODEL)),
        ],
        out_specs=[pl.BlockSpec((tm, D_MODEL), row), pl.BlockSpec((tm, D_MODEL), row)],
        out_shape=[jax.ShapeDtypeStruct((n, D_MODEL), F32)] * 2,
        compiler_params=_params(("arbitrary",)),
        name="memkv",
    )(m2d, g, wk, wv)


def _fox_prompt_kernel(q_ref, k_ref, v_ref, lf_ref, o_ref, ccol, crow, cpad, kb, vlo, vhi, *, nq, tq):
    pair = pl.program_id(1)
    lane = lax.broadcasted_iota(jnp.int32, (1, LANES), 1)
    low = lane < FOX_HEAD_DIM
    sub_low = lax.broadcasted_iota(jnp.int32, (LANES, 1), 0) < FOX_HEAD_DIM

    @pl.when(pair == 0)
    def _cumsum():
        triu = _triu(tq)
        cpad[...] = jnp.zeros_like(cpad)
        carry = jnp.zeros((FOX_HEADS, 1), F32)
        for b in range(nq):
            c = _cumsum_lanes(lf_ref[0, :, b * tq:(b + 1) * tq], triu, carry)
            carry = c[:, tq - 1:tq]
            c2 = c * LOG2E
            crow[b] = c2
            cpad[0:FOX_HEADS, :] = c2
            ccol[b] = cpad[...].T

    for b in range(nq):
        cs = slice(b * tq, (b + 1) * tq)
        kb[:, cs] = k_ref[0, :, :, cs].reshape(LANES, tq).astype(BF16)
        vblk = v_ref[0, :, :, cs].reshape(LANES, tq)
        vlo[:, cs] = jnp.where(sub_low, vblk, 0.0).astype(BF16)
        vhi[:, cs] = jnp.where(sub_low, 0.0, vblk).astype(BF16)

    r_idx = lax.broadcasted_iota(jnp.int32, (tq, tq), 0)
    c_idx = lax.broadcasted_iota(jnp.int32, (tq, tq), 1)
    causal = r_idx >= c_idx
    zero = jnp.zeros((), BF16)
    for i in range(nq):
        qp = q_ref[0, i * tq:(i + 1) * tq, :]
        cblk = ccol[i]
        o_blk = None
        for odd in range(2):
            head = 2 * pair + odd
            q_h = jnp.where(low, zero, qp) if odd else jnp.where(low, qp, zero)
            v_h = vhi if odd else vlo
            cq = jnp.sum(jnp.where(lane == head, cblk, 0.0), axis=-1, keepdims=True)
            s_list = []
            for j in range(i + 1):
                s = _dot(q_h, kb[:, j * tq:(j + 1) * tq]) - crow[j, pl.ds(head, 1), :]
                if j == i:
                    s = jnp.where(causal, s, -jnp.inf)
                s_list.append(s)
            p_list, l = _softmax_stats_p(s_list, cq)
            p_cat = jnp.concatenate([p.astype(BF16) for p in p_list], axis=1)
            o_h = _dot_nt(p_cat, v_h[:, 0:(i + 1) * tq]) / l
            o_blk = o_h if o_blk is None else o_blk + o_h
        o_ref[0, i * tq:(i + 1) * tq, :] = o_blk.astype(BF16)


def _fox_prompt(q, k_t, v_t, lf_t, tq):
    bsz, t, _ = q.shape
    nq = t // tq
    col = lambda b, p: (b, 0, p)
    kv_spec = pl.BlockSpec((1, 2, FOX_HEAD_DIM, t), lambda b, p: (b, p, 0, 0))
    return pl.pallas_call(
        functools.partial(_fox_prompt_kernel, nq=nq, tq=tq),
        grid=(bsz, FOX_PAIRS),
        in_specs=[
            pl.BlockSpec((1, t, LANES), col),
            kv_spec,
            kv_spec,
            pl.BlockSpec((1, FOX_HEADS, t), lambda b, p: (b, 0, 0)),
        ],
        out_specs=pl.BlockSpec((1, t, LANES), col),
        out_shape=jax.ShapeDtypeStruct((bsz, t, D_FOX), BF16),
        scratch_shapes=[
            pltpu.VMEM((nq, tq, LANES), F32),
            pltpu.VMEM((nq, FOX_HEADS, tq), F32),
            pltpu.VMEM((LANES, tq), F32),
            pltpu.VMEM((LANES, t), BF16),
            pltpu.VMEM((LANES, t), BF16),
            pltpu.VMEM((LANES, t), BF16),
        ],
        compiler_params=_params(("arbitrary", "arbitrary")),
        name="fox_prompt",
    )(q, k_t, v_t, lf_t)


def _mix_sample_kernel(u_ref, q_ref, k_ref, v_ref, lf_ref, kc_ref, vc_ref, lfc_ref, ph_ref,
                       wpool_ref, pscale_ref, op_ref, of_ref, cbuf, crow, uext, tmp_a, tmp_b,
                       *, past, t, tc):
    nblk = past // tc
    triu = _triu(tc)
    carry = jnp.zeros((FOX_HEADS, 1), F32)
    for b in range(nblk):
        c = _cumsum_lanes(lfc_ref[0, :, b * tc:(b + 1) * tc], triu, carry)
        carry = c[:, tc - 1:tc]
        crow[:, b * tc:(b + 1) * tc] = c * LOG2E
    cbuf[...] = jnp.zeros_like(cbuf)
    cbuf[0:t, 0:FOX_HEADS] = lf_ref[0]
    cnew = _cumsum_rows(cbuf[...], _tril(LANES), jnp.zeros((1, LANES), F32))
    cnew_row = (cnew.T[0:FOX_HEADS, 0:t] + carry) * LOG2E

    uext[0:POOL_BASE - POOL_HIST, :] = jnp.zeros((POOL_BASE - POOL_HIST, D_POOL), F32)
    tmp_a[0:8, :] = jnp.zeros((8, POOL_GROUP_W), F32)
    tmp_b[0:8, :] = jnp.zeros((8, POOL_GROUP_W), F32)
    uext[POOL_BASE - POOL_HIST:POOL_BASE, :] = ph_ref[0]
    uext[POOL_BASE:POOL_BASE + t, :] = u_ref[0]
    _pool_mixer(uext, tmp_a, tmp_b, t, 0, past, wpool_ref, pscale_ref, op_ref.at[0])

    r_idx = lax.broadcasted_iota(jnp.int32, (t, t), 0)
    c_idx = lax.broadcasted_iota(jnp.int32, (t, t), 1)
    causal = r_idx >= c_idx
    for h in range(FOX_HEADS):
        hs = slice(h * FOX_HEAD_DIM, (h + 1) * FOX_HEAD_DIM)
        q_h = q_ref[0, :, hs]
        cq = (cnew[0:t, h:h + 1] + carry[h:h + 1, :]) * LOG2E
        s_hist = _dot(q_h, kc_ref[0, h].astype(BF16)) - crow[h:h + 1, :]
        s_new = _dot_nt(q_h, k_ref[0, :, hs].astype(BF16)) - cnew_row[h:h + 1, :]
        s_new = jnp.where(causal, s_new, -jnp.inf)
        m_row = jnp.maximum(jnp.max(s_hist, axis=-1, keepdims=True),
                            jnp.max(s_new, axis=-1, keepdims=True)) + cq
        r = cq - m_row
        p_hist = jnp.exp2(s_hist + r)
        p_new = jnp.exp2(s_new + r)
        l = jnp.sum(p_hist, axis=-1, keepdims=True) + jnp.sum(p_new, axis=-1, keepdims=True)
        acc = (_dot_nt(p_hist.astype(BF16), vc_ref[0, h].astype(BF16))
               + _dot(p_new.astype(BF16), v_ref[0, :, hs].astype(BF16)))
        of_ref[0, :, hs] = (acc / l).astype(BF16)


def _mix_sample(u, q, k, v, lf, kc_t, vc_t, lfc_t, ph, w_pool, pool_scale, tc):
    bsz, t, _ = u.shape
    past = kc_t.shape[3]
    assert t <= LANES and past % tc == 0 and t >= POOL_HIST
    one = lambda b: (b, 0, 0)
    cache_spec = pl.BlockSpec((1, FOX_HEADS, FOX_HEAD_DIM, past), lambda b: (b, 0, 0, 0))
    kern = functools.partial(_mix_sample_kernel, past=past, t=t, tc=tc)
    return pl.pallas_call(
        kern,
        grid=(bsz,),
        in_specs=[
            pl.BlockSpec((1, t, D_POOL), one),
            pl.BlockSpec((1, t, D_FOX), one),
            pl.BlockSpec((1, t, D_FOX), one),
            pl.BlockSpec((1, t, D_FOX), one),
            pl.BlockSpec((1, t, FOX_HEADS), one),
            cache_spec,
            cache_spec,
            pl.BlockSpec((1, FOX_HEADS, past), one),
            pl.BlockSpec((1, POOL_HIST, D_POOL), one),
            _const_spec((len(POOL_WINDOWS), POOL_GROUP_W, POOL_GROUP_W)),
            _const_spec((1, D_POOL)),
        ],
        out_specs=[pl.BlockSpec((1, t, D_POOL), one), pl.BlockSpec((1, t, D_FOX), one)],
        out_shape=[jax.ShapeDtypeStruct((bsz, t, D_POOL), BF16),
                   jax.ShapeDtypeStruct((bsz, t, D_FOX), BF16)],
        scratch_shapes=[
            pltpu.VMEM((LANES, LANES), F32),
            pltpu.VMEM((FOX_HEADS, past), F32),
            pltpu.VMEM((POOL_BASE + t, D_POOL), F32),
            pltpu.VMEM((POOL_BASE + t, POOL_GROUP_W), F32),
            pltpu.VMEM((POOL_BASE + t, POOL_GROUP_W), F32),
        ],
        compiler_params=_params(("arbitrary",)),
        name="mix_sample",
    )(u, q, k, v, lf, kc_t, vc_t, lfc_t, ph, w_pool, pool_scale)


def _memffn_kernel(x_ref, op_ref, of_ref, mk_ref, mv_ref, ch_ref, wo_ref, gxq_ref, wmq_ref, wmo_ref,
                   gffn_ref, wup_ref, cw_ref, cb_ref, wdown_ref, gfin_ref, y_ref, cs_ref, aext, mo_sc,
                   *, tm, final_norm):
    i = pl.program_id(1)
    x = x_ref[0] + _dot(op_ref[0], wo_ref[0:D_POOL, :]) + _dot(of_ref[0], wo_ref[D_POOL:D_MODEL, :])

    hq = _rms(x, gxq_ref[...]).astype(BF16)
    mq = (_dot(hq, wmq_ref[...]) * (MEM_HEAD_DIM ** -0.5)).astype(BF16)
    for h in range(MEM_HEADS):
        hs = slice(h * MEM_HEAD_DIM, (h + 1) * MEM_HEAD_DIM)
        s = _dot_nt(mq[:, hs], mk_ref[0, :, hs].astype(BF16))
        m = jnp.max(s, axis=-1, keepdims=True)
        p = jnp.exp(s - m)
        l = jnp.sum(p, axis=-1, keepdims=True)
        o = _dot(p.astype(BF16), mv_ref[0, :, hs].astype(BF16)) / l
        mo_sc[:, hs] = o.astype(BF16)
    x2 = x + _dot(mo_sc[...], wmo_ref[...])

    @pl.when(i == 0)
    def _hist():
        aext[CONV_PAD - (CONV_W - 1):CONV_PAD, :] = ch_ref[0]

    hf = _rms(x2, gffn_ref[...]).astype(BF16)
    acc = jnp.zeros((tm, D_MODEL), F32)
    for c0, c1 in FF_CHUNKS:
        cs = slice(c0, c1)
        a = _dot(hf, wup_ref[:, cs])
        b = _dot(hf, wup_ref[:, D_FF + c0:D_FF + c1])
        aext[CONV_PAD:CONV_PAD + tm, cs] = a
        conv = (aext[CONV_PAD - 2:CONV_PAD - 2 + tm, cs] * cw_ref[0:1, cs]
                + aext[CONV_PAD - 1:CONV_PAD - 1 + tm, cs] * cw_ref[1:2, cs]
                + a * cw_ref[2:3, cs])
        conv = cb_ref[:, cs] + conv
        gate = conv * (1.0 / (1.0 + jnp.exp(-conv)))
        acc = acc + _dot((gate * b).astype(BF16), wdown_ref[cs, :])
    cs_ref[0] = aext[CONV_PAD + tm - (CONV_W - 1):CONV_PAD + tm, :]
    aext[0:CONV_PAD, :] = aext[tm:tm + CONV_PAD, :]
    x3 = x2 + acc
    if final_norm:
        x3 = _rms(x3, gfin_ref[...])
    y_ref[0] = x3


def _memffn(x, o_pool, o_fox, mk, mv, conv_hist, w_o, g_xq, w_mq, w_mo, g_ffn, w_up, conv_w, conv_b,
            w_down, g_final, tm, final_norm):
    bsz, t, _ = x.shape
    blk = lambda b, i: (b, i, 0)
    full = lambda b, i: (b, 0, 0)
    kern = functools.partial(_memffn_kernel, tm=tm, final_norm=final_norm)
    return pl.pallas_call(
        kern,
        grid=(bsz, t // tm),
        in_specs=[
            pl.BlockSpec((1, tm, D_MODEL), blk),
            pl.BlockSpec((1, tm, D_POOL), blk),
            pl.BlockSpec((1, tm, D_FOX), blk),
            pl.BlockSpec((1, N_MEM, D_MODEL), full),
            pl.BlockSpec((1, N_MEM, D_MODEL), full),
            pl.BlockSpec((1, CONV_W - 1, D_FF), full),
            _const_spec((D_MODEL, D_MODEL)),
            _const_spec((1, D_MODEL)),
            _const_spec((D_MODEL, D_MODEL)),
            _const_spec((D_MODEL, D_MODEL)),
            _const_spec((1, D_MODEL)),
            _const_spec((D_MODEL, 2 * D_FF)),
            _const_spec((CONV_W, D_FF)),
            _const_spec((1, D_FF)),
            _const_spec((D_FF, D_MODEL)),
            _const_spec((1, D_MODEL)),
        ],
        out_specs=[
            pl.BlockSpec((1, tm, D_MODEL), blk),
            pl.BlockSpec((1, CONV_W - 1, D_FF), full),
        ],
        out_shape=[
            jax.ShapeDtypeStruct((bsz, t, D_MODEL), F32),
            jax.ShapeDtypeStruct((bsz, CONV_W - 1, D_FF), F32),
        ],
        scratch_shapes=[
            pltpu.VMEM((CONV_PAD + tm, D_FF), F32),
            pltpu.VMEM((tm, D_MODEL), BF16),
        ],
        compiler_params=_params(("arbitrary", "arbitrary")),
        name="memffn",
    )(x, o_pool, o_fox, mk, mv, conv_hist, w_o, g_xq, w_mq, w_mo, g_ffn, w_up, conv_w, conv_b, w_down,
      g_final)


def _row_tile(n, target):
    tm = min(n, target)
    assert n % tm == 0
    return tm


def kernel(x_prompt, x_sample, cache_fox_k, cache_fox_v, cache_fox_logf, state_pool, state_ffn_conv, cache_mem_k, cache_mem_v, mem_prompt, g_mix, w_in, b_f, w_pool, pool_scale, w_o, g_xq, g_mkv, w_mq, w_mk, w_mv, w_mo, g_ffn, w_up, conv_w, conv_b, w_down, g_final):
    depth = w_in.shape[0]
    bp, tp, _ = x_prompt.shape
    bs, ts, _ = x_sample.shape
    past = cache_fox_k.shape[2]
    n_main = D_POOL + 3 * D_FOX

    xp, xs = x_prompt, x_sample
    outs = {name: [] for name in ("pk", "pv", "plf", "ppool", "pconv", "pmk", "pmv",
                                   "sk", "sv", "slf", "spool", "sconv")}
    for l in range(depth):
        last = l == depth - 1
        row2 = lambda a: a.reshape(1, -1).astype(F32)
        w_main = w_in[l][:, :n_main].astype(BF16)
        w_f = jnp.pad(w_in[l][:, n_main:], ((0, 0), (0, LANES - FOX_HEADS))).astype(BF16)
        bf = jnp.pad(b_f[l].astype(F32), (0, LANES - FOX_HEADS)).reshape(1, LANES)
        wp = w_pool[l].astype(BF16)
        ps = row2(pool_scale[l])
        ffn_w = (w_o[l].astype(BF16), row2(g_xq[l]), w_mq[l].astype(BF16), w_mo[l].astype(BF16),
                 row2(g_ffn[l]), w_up[l].astype(BF16), conv_w[l].astype(F32), row2(conv_b[l]),
                 w_down[l].astype(BF16), row2(g_final))

        mk, mv = _memkv(mem_prompt.reshape(bp * N_MEM, D_MODEL), row2(g_mkv[l]),
                        w_mk[l].astype(BF16), w_mv[l].astype(BF16), _row_tile(bp * N_MEM, 512))
        mk = mk.reshape(bp, N_MEM, D_MODEL)
        mv = mv.reshape(bp, N_MEM, D_MODEL)
        o_pool, u_tail, q, k_t, v_t, lf_t = _inproj(xp.reshape(bp * tp, D_MODEL), row2(g_mix[l]), w_main,
                                                    w_f, bf, wp, ps, _row_tile(tp, 512), tp, pool=True)
        o_fox = _fox_prompt(q.reshape(bp, tp, D_FOX), k_t, v_t, lf_t, _row_tile(tp, 256))
        xp, pconv = _memffn(xp, o_pool.reshape(bp, tp, D_POOL), o_fox, mk, mv,
                            jnp.zeros((bp, CONV_W - 1, D_FF), F32), *ffn_w,
                            tm=_row_tile(tp, 256), final_norm=last)
        outs["pk"].append(jnp.transpose(k_t, (0, 3, 1, 2)))
        outs["pv"].append(jnp.transpose(v_t, (0, 3, 1, 2)))
        outs["plf"].append(jnp.transpose(lf_t, (0, 2, 1)))
        outs["ppool"].append(u_tail[:, POOL_CARRY - POOL_HIST:, :])
        outs["pconv"].append(pconv)
        outs["pmk"].append(mk.reshape(bp, N_MEM, MEM_HEADS, MEM_HEAD_DIM))
        outs["pmv"].append(mv.reshape(bp, N_MEM, MEM_HEADS, MEM_HEAD_DIM))

        u, q, k, v, lf = _inproj(xs.reshape(bs * ts, D_MODEL), row2(g_mix[l]), w_main, w_f, bf,
                                 wp, ps, _row_tile(bs * ts, 512), ts, pool=False)
        u = u.reshape(bs, ts, D_POOL)
        k = k.reshape(bs, ts, D_FOX)
        v = v.reshape(bs, ts, D_FOX)
        lf = lf.reshape(bs, ts, FOX_HEADS)
        o_pool, o_fox = _mix_sample(u, q.reshape(bs, ts, D_FOX), k, v, lf,
                                    jnp.transpose(cache_fox_k[l], (0, 2, 3, 1)),
                                    jnp.transpose(cache_fox_v[l], (0, 2, 3, 1)),
                                    jnp.transpose(cache_fox_logf[l].astype(F32), (0, 2, 1)),
                                    state_pool[l], wp, ps, tc=512)
        xs, sconv = _memffn(xs, o_pool, o_fox, cache_mem_k[l].reshape(bs, N_MEM, D_MODEL),
                            cache_mem_v[l].reshape(bs, N_MEM, D_MODEL), state_ffn_conv[l], *ffn_w,
                            tm=ts, final_norm=last)
        outs["sk"].append(k.reshape(bs, ts, FOX_HEADS, FOX_HEAD_DIM))
        outs["sv"].append(v.reshape(bs, ts, FOX_HEADS, FOX_HEAD_DIM))
        outs["slf"].append(lf)
        outs["spool"].append(u[:, ts - POOL_HIST:, :])
        outs["sconv"].append(sconv)

    st = {name: jnp.stack(vals) for name, vals in outs.items()}
    return (xp, xs, st["pk"], st["pv"], st["plf"], st["ppool"], st["pconv"], st["pmk"], st["pmv"],
            st["sk"], st["sv"], st["slf"], st["spool"], st["sconv"])
```

```python
import functools
import math

import jax
import jax.numpy as jnp
from jax import lax
from jax.experimental import pallas as pl
from jax.experimental.pallas import tpu as pltpu

D_MODEL = 1024
D_POOL = 512
POOL_WINDOWS = (2, 4, 8, 16)
POOL_GROUP_W = 128
POOL_HIST = 15
POOL_CARRY = 16
POOL_BASE = 24
D_FOX = 512
FOX_HEADS = 8
FOX_HEAD_DIM = 64
FOX_PAIRS = FOX_HEADS // 2
N_MEM = 256
MEM_HEADS = 4
MEM_HEAD_DIM = 256
D_FF = 2816
CONV_W = 3
CONV_PAD = 8
FF_CHUNKS = ((0, 1024), (1024, 2048), (2048, 2816))
EPS = 1e-6
LOG2E = math.log2(math.e)
LANES = 128
VMEM_LIMIT = 60 * 1024 * 1024

F32 = jnp.float32
BF16 = jnp.bfloat16


def _dot(a, b):
    return jnp.dot(a, b, preferred_element_type=F32)


def _dot_nt(a, b):
    return lax.dot_general(a, b, (((1,), (1,)), ((), ())), preferred_element_type=F32)


def _rms(x, g):
    ms = jnp.mean(x * x, axis=-1, keepdims=True)
    return (x * lax.rsqrt(ms + EPS)) * g


def _const_spec(shape):
    nd = len(shape)
    return pl.BlockSpec(shape, lambda *_: (0,) * nd, pipeline_mode=pl.Buffered(1))


def _params(sem):
    return pltpu.CompilerParams(dimension_semantics=sem, vmem_limit_bytes=VMEM_LIMIT)


def _tril(n):
    r = lax.broadcasted_iota(jnp.int32, (n, n), 0)
    c = lax.broadcasted_iota(jnp.int32, (n, n), 1)
    return jnp.where(r >= c, 1.0, 0.0).astype(BF16)


def _triu(n):
    r = lax.broadcasted_iota(jnp.int32, (n, n), 0)
    c = lax.broadcasted_iota(jnp.int32, (n, n), 1)
    return jnp.where(r <= c, 1.0, 0.0).astype(BF16)


def _split3(x):
    hi = x.astype(BF16)
    r1 = x - hi.astype(F32)
    mid = r1.astype(BF16)
    lo = (r1 - mid.astype(F32)).astype(BF16)
    return hi, mid, lo


def _cumsum_rows(x, tril, carry):
    hi, mid, lo = _split3(x)
    return _dot(tril, hi) + _dot(tril, mid) + _dot(tril, lo) + carry


def _cumsum_lanes(x, triu, carry):
    hi, mid, lo = _split3(x)
    return _dot(hi, triu) + _dot(mid, triu) + _dot(lo, triu) + carry


def _pool_mixer(uext_ref, tmp_a, tmp_b, tq, t0, pos0, wpool_ref, pscale_ref, out_ref):
    lo, hi = 8, POOL_BASE + tq
    t_idx = t0 + lax.broadcasted_iota(jnp.int32, (tq, 1), 0)
    for g, w in enumerate(POOL_WINDOWS):
        sl = slice(g * POOL_GROUP_W, (g + 1) * POOL_GROUP_W)
        src, bufs, d = uext_ref, (tmp_a, tmp_b), 1
        col = sl
        while 2 * d < w:
            dst = bufs[0]
            dst[lo:hi, :] = src[lo:hi, col] + src[lo - d:hi - d, col]
            src, bufs, d, col = dst, (bufs[1], bufs[0]), 2 * d, slice(0, POOL_GROUP_W)
        s = src[POOL_BASE:hi, col] + src[POOL_BASE - d:hi - d, col]
        u = uext_ref[POOL_BASE:hi, sl]
        count = jnp.minimum(pos0 + t_idx + 1, w).astype(F32)
        diff = (s / count - u).astype(BF16)
        y = _dot(diff, wpool_ref[g]) * pscale_ref[:, sl]
        out_ref[:, sl] = y.astype(out_ref.dtype)


def _fold_lanes(x, op):
    parts = [x[:, c:c + LANES] for c in range(0, x.shape[1], LANES)]
    return functools.reduce(op, parts)


def _inproj_kernel(x_ref, g_ref, w_ref, wf_ref, bf_ref, wpool_ref, pscale_ref, *refs, tm, sub, bps, pool):
    if pool:
        op_ref, ut_ref, q_ref, k_ref, v_ref, lf_ref, kb_ref, vb_ref, uext, tmp_a, tmp_b = refs
    else:
        u_ref, q_ref, k_ref, v_ref, lf_ref = refs
    if pool:
        blk = pl.program_id(0) % bps

        @pl.when(blk == 0)
        def _no_history():
            uext[0:POOL_BASE, :] = jnp.zeros((POOL_BASE, D_POOL), F32)
            tmp_a[0:8, :] = jnp.zeros((8, POOL_GROUP_W), F32)
            tmp_b[0:8, :] = jnp.zeros((8, POOL_GROUP_W), F32)

    subs = [slice(r0, r0 + sub) for r0 in range(0, tm, sub)]
    hs = [_rms(x_ref[rs, :], g_ref[...]).astype(BF16) for rs in subs]
    for rs, h in zip(subs, hs):
        u = _dot(h, w_ref[:, 0:D_POOL])
        if pool:
            uext[POOL_BASE + rs.start:POOL_BASE + rs.stop, :] = u
        else:
            u_ref[rs, :] = u
    if pool:
        _pool_mixer(uext, tmp_a, tmp_b, tm, blk * tm, 0, wpool_ref, pscale_ref, op_ref)
        tail = uext[tm + POOL_BASE - POOL_CARRY:tm + POOL_BASE, :]
        ut_ref[0] = tail
        uext[POOL_BASE - POOL_CARRY:POOL_BASE, :] = tail
    for rs, h in zip(subs, hs):
        q_ref[rs, :] = (_dot(h, w_ref[:, D_POOL:D_POOL + D_FOX]) * (FOX_HEAD_DIM ** -0.5 * LOG2E)).astype(BF16)
        k = _dot(h, w_ref[:, D_POOL + D_FOX:D_POOL + 2 * D_FOX])
        v = _dot(h, w_ref[:, D_POOL + 2 * D_FOX:D_POOL + 3 * D_FOX])
        zf = _dot(h, wf_ref[...]) + bf_ref[...]
        lf = -(jnp.maximum(-zf, 0.0) + jnp.log1p(jnp.exp(-jnp.abs(zf))))
        if pool:
            k_t = k.T.reshape(FOX_HEADS, FOX_HEAD_DIM, sub)
            v_t = v.T.reshape(FOX_HEADS, FOX_HEAD_DIM, sub)
            k_ref[0, :, :, rs] = k_t
            v_ref[0, :, :, rs] = v_t
            kb_ref[0, :, :, rs] = k_t.astype(BF16)
            vb_ref[0, :, :, rs] = v_t.astype(BF16)
            lf_ref[0, :, rs] = lf.T[0:FOX_HEADS, :]
        else:
            k_ref[rs, :] = k
            v_ref[rs, :] = v
            lf_ref[rs, :] = lf[:, 0:FOX_HEADS]


def _inproj(x2d, g, w_main, w_f, b_f, w_pool, pool_scale, tm, seq_len, pool):
    n = x2d.shape[0]
    bps = seq_len // tm
    row = lambda i: (i, 0)
    if pool:
        nb = n // seq_len
        kv_spec = pl.BlockSpec((1, FOX_HEADS, FOX_HEAD_DIM, tm), lambda i: (i // bps, 0, 0, i % bps))
        kv_shape = jax.ShapeDtypeStruct((nb, FOX_HEADS, FOX_HEAD_DIM, seq_len), F32)
        kvb_shape = jax.ShapeDtypeStruct((nb, FOX_HEADS, FOX_HEAD_DIM, seq_len), BF16)
        out_specs = [pl.BlockSpec((tm, D_POOL), row),
                     pl.BlockSpec((1, POOL_CARRY, D_POOL), lambda i: (i // bps, 0, 0)),
                     pl.BlockSpec((tm, D_FOX), row), kv_spec, kv_spec,
                     pl.BlockSpec((1, FOX_HEADS, tm), lambda i: (i // bps, 0, i % bps)),
                     kv_spec, kv_spec]
        out_shape = [jax.ShapeDtypeStruct((n, D_POOL), BF16),
                     jax.ShapeDtypeStruct((nb, POOL_CARRY, D_POOL), F32),
                     jax.ShapeDtypeStruct((n, D_FOX), BF16), kv_shape, kv_shape,
                     jax.ShapeDtypeStruct((nb, FOX_HEADS, seq_len), F32),
                     kvb_shape, kvb_shape]
        scratch = [pltpu.VMEM((POOL_BASE + tm, D_POOL), F32),
                   pltpu.VMEM((POOL_BASE + tm, POOL_GROUP_W), F32),
                   pltpu.VMEM((POOL_BASE + tm, POOL_GROUP_W), F32)]
    else:
        out_specs = [pl.BlockSpec((tm, D_POOL), row), pl.BlockSpec((tm, D_FOX), row),
                     pl.BlockSpec((tm, D_FOX), row), pl.BlockSpec((tm, D_FOX), row),
                     pl.BlockSpec((tm, FOX_HEADS), row)]
        out_shape = [jax.ShapeDtypeStruct((n, D_POOL), F32), jax.ShapeDtypeStruct((n, D_FOX), BF16),
                     jax.ShapeDtypeStruct((n, D_FOX), F32), jax.ShapeDtypeStruct((n, D_FOX), F32),
                     jax.ShapeDtypeStruct((n, FOX_HEADS), F32)]
        scratch = []
    return pl.pallas_call(
        functools.partial(_inproj_kernel, tm=tm, sub=min(tm, 512), bps=bps, pool=pool),
        grid=(n // tm,),
        in_specs=[
            pl.BlockSpec((tm, D_MODEL), row),
            _const_spec((1, D_MODEL)),
            _const_spec((D_MODEL, D_POOL + 3 * D_FOX)),
            _const_spec((D_MODEL, LANES)),
            _const_spec((1, LANES)),
            _const_spec((len(POOL_WINDOWS), POOL_GROUP_W, POOL_GROUP_W)),
            _const_spec((1, D_POOL)),
        ],
        out_specs=out_specs,
        out_shape=out_shape,
        scratch_shapes=scratch,
        compiler_params=_params(("arbitrary",)),
        name="inproj_pool" if pool else "inproj",
    )(x2d, g, w_main, w_f, b_f, w_pool, pool_scale)


def _memkv_kernel(m_ref, g_ref, wk_ref, wv_ref, mk_ref, mv_ref):
    h = _rms(m_ref[...], g_ref[...]).astype(BF16)
    mk_ref[...] = _dot(h, wk_ref[...])
    mv_ref[...] = _dot(h, wv_ref[...])


def _memkv(m2d, g, wk, wv, tm):
    n = m2d.shape[0]
    row = lambda i: (i, 0)
    return pl.pallas_call(
        _memkv_kernel,
        grid=(n // tm,),
        in_specs=[
            pl.BlockSpec((tm, D_MODEL), row),
            _const_spec((1, D_MODEL)),
            _const_spec((D_MODEL, D_MODEL)),
            _const_spec((D_MODEL, D_MODEL)),
        ],
        out_specs=[pl.BlockSpec((tm, D_MODEL), row), pl.BlockSpec((tm, D_MODEL), row)],
        out_shape=[jax.ShapeDtypeStruct((n, D_MODEL), F32)] * 2,
        compiler_params=_params(("arbitrary",)),
        name="memkv",
    )(m2d, g, wk, wv)


def _fox_prompt_kernel(q_ref, k_ref, v_ref, lf_ref, o_ref, ccol, crow, cpad, s_sc, p_sc, *, nq, tq):
    pair = pl.program_id(1)
    lane = lax.broadcasted_iota(jnp.int32, (1, LANES), 1)
    low = lane < FOX_HEAD_DIM

    @pl.when(pair == 0)
    def _cumsum():
        triu = _triu(tq)
        cpad[...] = jnp.zeros_like(cpad)
        carry = jnp.zeros((FOX_HEADS, 1), F32)
        for b in range(nq):
            c = _cumsum_lanes(lf_ref[0, :, b * tq:(b + 1) * tq], triu, carry)
            carry = c[:, tq - 1:tq]
            c2 = c * LOG2E
            crow[b] = c2
            cpad[0:FOX_HEADS, :] = c2
            ccol[b] = cpad[...].T

    r_idx = lax.broadcasted_iota(jnp.int32, (tq, tq), 0)
    c_idx = lax.broadcasted_iota(jnp.int32, (tq, tq), 1)
    causal = r_idx >= c_idx
    zero = jnp.zeros((), BF16)
    for i in reversed(range(nq)):
        keys = (i + 1) * tq
        qp = q_ref[0, i * tq:(i + 1) * tq, :]
        cblk = ccol[i]
        o_blk = None
        for odd in range(2):
            head = 2 * pair + odd
            q_h = jnp.where(low, zero, qp) if odd else jnp.where(low, qp, zero)
            v_own = v_ref[0, odd, :, 0:keys]
            v_pad = jnp.zeros_like(v_own)
            v_h = jnp.concatenate([v_pad, v_own] if odd else [v_own, v_pad], axis=0)
            cq = jnp.sum(jnp.where(lane == head, cblk, 0.0), axis=-1, keepdims=True)
            slot = 2 * (i % 2) + odd
            m = None
            for j in range(i + 1):
                k_j = k_ref[0, :, :, j * tq:(j + 1) * tq].reshape(LANES, tq)
                s = _dot(q_h, k_j) - crow[j, pl.ds(head, 1), :]
                if j == i:
                    s = jnp.where(causal, s, -jnp.inf)
                s_sc[slot, :, j * tq:(j + 1) * tq] = s
                mb = _fold_lanes(s, jnp.maximum)
                m = mb if m is None else jnp.maximum(m, mb)
            m_row = jnp.max(m, axis=-1, keepdims=True) + cq
            r = cq - m_row
            l = None
            for j in range(i + 1):
                p = jnp.exp2(s_sc[slot, :, j * tq:(j + 1) * tq] + r)
                p_sc[slot, :, j * tq:(j + 1) * tq] = p.astype(BF16)
                lb = _fold_lanes(p, jnp.add)
                l = lb if l is None else l + lb
            l = jnp.sum(l, axis=-1, keepdims=True)
            o_h = _dot_nt(p_sc[slot, :, 0:keys], v_h) / l
            o_blk = o_h if o_blk is None else o_blk + o_h
        o_ref[0, i * tq:(i + 1) * tq, :] = o_blk.astype(BF16)


def _fox_prompt(q, k_t, v_t, lf_t, tq):
    bsz, t, _ = q.shape
    nq = t // tq
    col = lambda b, p: (b, 0, p)
    kv_spec = pl.BlockSpec((1, 2, FOX_HEAD_DIM, t), lambda b, p: (b, p, 0, 0))
    return pl.pallas_call(
        functools.partial(_fox_prompt_kernel, nq=nq, tq=tq),
        grid=(bsz, FOX_PAIRS),
        in_specs=[
            pl.BlockSpec((1, t, LANES), col),
            kv_spec,
            kv_spec,
            pl.BlockSpec((1, FOX_HEADS, t), lambda b, p: (b, 0, 0)),
        ],
        out_specs=pl.BlockSpec((1, t, LANES), col),
        out_shape=jax.ShapeDtypeStruct((bsz, t, D_FOX), BF16),
        scratch_shapes=[
            pltpu.VMEM((nq, tq, LANES), F32),
            pltpu.VMEM((nq, FOX_HEADS, tq), F32),
            pltpu.VMEM((LANES, tq), F32),
            pltpu.VMEM((4, tq, t), F32),
            pltpu.VMEM((4, tq, t), BF16),
        ],
        compiler_params=_params(("arbitrary", "arbitrary")),
        name="fox_prompt",
    )(q, k_t, v_t, lf_t)


def _mix_sample_kernel(u_ref, q_ref, k_ref, v_ref, lf_ref, kc_ref, vc_ref, lfc_ref, ph_ref,
                       wpool_ref, pscale_ref, op_ref, of_ref, cbuf, crow, uext, tmp_a, tmp_b,
                       *, past, t, tc):
    nblk = past // tc
    triu = _triu(tc)
    carry = jnp.zeros((FOX_HEADS, 1), F32)
    for b in range(nblk):
        c = _cumsum_lanes(lfc_ref[0, :, b * tc:(b + 1) * tc], triu, carry)
        carry = c[:, tc - 1:tc]
        crow[:, b * tc:(b + 1) * tc] = c * LOG2E
    cbuf[...] = jnp.zeros_like(cbuf)
    cbuf[0:t, 0:FOX_HEADS] = lf_ref[0]
    cnew = _cumsum_rows(cbuf[...], _tril(LANES), jnp.zeros((1, LANES), F32))
    cnew_row = (cnew.T[0:FOX_HEADS, 0:t] + carry) * LOG2E

    uext[0:POOL_BASE - POOL_HIST, :] = jnp.zeros((POOL_BASE - POOL_HIST, D_POOL), F32)
    tmp_a[0:8, :] = jnp.zeros((8, POOL_GROUP_W), F32)
    tmp_b[0:8, :] = jnp.zeros((8, POOL_GROUP_W), F32)
    uext[POOL_BASE - POOL_HIST:POOL_BASE, :] = ph_ref[0]
    uext[POOL_BASE:POOL_BASE + t, :] = u_ref[0]
    _pool_mixer(uext, tmp_a, tmp_b, t, 0, past, wpool_ref, pscale_ref, op_ref.at[0])

    r_idx = lax.broadcasted_iota(jnp.int32, (t, t), 0)
    c_idx = lax.broadcasted_iota(jnp.int32, (t, t), 1)
    causal = r_idx >= c_idx
    for h in range(FOX_HEADS):
        hs = slice(h * FOX_HEAD_DIM, (h + 1) * FOX_HEAD_DIM)
        q_h = q_ref[0, :, hs]
        cq = (cnew[0:t, h:h + 1] + carry[h:h + 1, :]) * LOG2E
        s_hist = _dot(q_h, kc_ref[0, h].astype(BF16)) - crow[h:h + 1, :]
        s_new = _dot_nt(q_h, k_ref[0, :, hs].astype(BF16)) - cnew_row[h:h + 1, :]
        s_new = jnp.where(causal, s_new, -jnp.inf)
        m_row = jnp.maximum(jnp.max(s_hist, axis=-1, keepdims=True),
                            jnp.max(s_new, axis=-1, keepdims=True)) + cq
        r = cq - m_row
        p_hist = jnp.exp2(s_hist + r)
        p_new = jnp.exp2(s_new + r)
        l = jnp.sum(p_hist, axis=-1, keepdims=True) + jnp.sum(p_new, axis=-1, keepdims=True)
        acc = (_dot_nt(p_hist.astype(BF16), vc_ref[0, h].astype(BF16))
               + _dot(p_new.astype(BF16), v_ref[0, :, hs].astype(BF16)))
        of_ref[0, :, hs] = (acc / l).astype(BF16)


def _mix_sample(u, q, k, v, lf, kc_t, vc_t, lfc_t, ph, w_pool, pool_scale, tc):
    bsz, t, _ = u.shape
    past = kc_t.shape[3]
    assert t <= LANES and past % tc == 0 and t >= POOL_HIST
    one = lambda b: (b, 0, 0)
    cache_spec = pl.BlockSpec((1, FOX_HEADS, FOX_HEAD_DIM, past), lambda b: (b, 0, 0, 0))
    kern = functools.partial(_mix_sample_kernel, past=past, t=t, tc=tc)
    return pl.pallas_call(
        kern,
        grid=(bsz,),
        in_specs=[
            pl.BlockSpec((1, t, D_POOL), one),
            pl.BlockSpec((1, t, D_FOX), one),
            pl.BlockSpec((1, t, D_FOX), one),
            pl.BlockSpec((1, t, D_FOX), one),
            pl.BlockSpec((1, t, FOX_HEADS), one),
            cache_spec,
            cache_spec,
            pl.BlockSpec((1, FOX_HEADS, past), one),
            pl.BlockSpec((1, POOL_HIST, D_POOL), one),
            _const_spec((len(POOL_WINDOWS), POOL_GROUP_W, POOL_GROUP_W)),
            _const_spec((1, D_POOL)),
        ],
        out_specs=[pl.BlockSpec((1, t, D_POOL), one), pl.BlockSpec((1, t, D_FOX), one)],
        out_shape=[jax.ShapeDtypeStruct((bsz, t, D_POOL), BF16),
                   jax.ShapeDtypeStruct((bsz, t, D_FOX), BF16)],
        scratch_shapes=[
            pltpu.VMEM((LANES, LANES), F32),
            pltpu.VMEM((FOX_HEADS, past), F32),
            pltpu.VMEM((POOL_BASE + t, D_POOL), F32),
            pltpu.VMEM((POOL_BASE + t, POOL_GROUP_W), F32),
            pltpu.VMEM((POOL_BASE + t, POOL_GROUP_W), F32),
        ],
        compiler_params=_params(("arbitrary",)),
        name="mix_sample",
    )(u, q, k, v, lf, kc_t, vc_t, lfc_t, ph, w_pool, pool_scale)


def _memffn_kernel(x_ref, op_ref, of_ref, mk_ref, mv_ref, ch_ref, wo_ref, gxq_ref, wmq_ref, wmo_ref,
                   gffn_ref, wup_ref, cw_ref, cb_ref, wdown_ref, gfin_ref, y_ref, cs_ref, aext, mo_sc,
                   *, tm, sub, final_norm):
    @pl.when(pl.program_id(1) == 0)
    def _hist():
        aext[CONV_PAD - (CONV_W - 1):CONV_PAD, :] = ch_ref[0]

    subs = [slice(r0, r0 + sub) for r0 in range(0, tm, sub)]
    xs = [x_ref[0, rs, :] + _dot(op_ref[0, rs, :], wo_ref[0:D_POOL, :])
          + _dot(of_ref[0, rs, :], wo_ref[D_POOL:D_MODEL, :]) for rs in subs]

    mqs = [(_dot(_rms(x, gxq_ref[...]).astype(BF16), wmq_ref[...]) * (MEM_HEAD_DIM ** -0.5)).astype(BF16)
           for x in xs]
    for h in range(MEM_HEADS):
        hs = slice(h * MEM_HEAD_DIM, (h + 1) * MEM_HEAD_DIM)
        mk_h = mk_ref[0, :, hs].astype(BF16)
        mv_h = mv_ref[0, :, hs].astype(BF16)
        for rs, mq in zip(subs, mqs):
            s = _dot_nt(mq[:, hs], mk_h)
            m = jnp.max(s, axis=-1, keepdims=True)
            p = jnp.exp(s - m)
            l = jnp.sum(p, axis=-1, keepdims=True)
            o = _dot(p.astype(BF16), mv_h) / l
            mo_sc[rs, hs] = o.astype(BF16)
    x2s = [x + _dot(mo_sc[rs, :], wmo_ref[...]) for rs, x in zip(subs, xs)]

    hfs = [_rms(x2, gffn_ref[...]).astype(BF16) for x2 in x2s]
    accs = [jnp.zeros((sub, D_MODEL), F32) for _ in subs]
    for c0, c1 in FF_CHUNKS:
        cs = slice(c0, c1)
        for n, (rs, hf) in enumerate(zip(subs, hfs)):
            a = _dot(hf, wup_ref[:, cs])
            b = _dot(hf, wup_ref[:, D_FF + c0:D_FF + c1])
            r0, r1 = CONV_PAD + rs.start, CONV_PAD + rs.stop
            aext[r0:r1, cs] = a
            conv = (aext[r0 - 2:r1 - 2, cs] * cw_ref[0:1, cs]
                    + aext[r0 - 1:r1 - 1, cs] * cw_ref[1:2, cs]
                    + a * cw_ref[2:3, cs])
            conv = cb_ref[:, cs] + conv
            gate = conv * (1.0 / (1.0 + jnp.exp(-conv)))
            accs[n] = accs[n] + _dot((gate * b).astype(BF16), wdown_ref[cs, :])
    cs_ref[0] = aext[CONV_PAD + tm - (CONV_W - 1):CONV_PAD + tm, :]
    aext[0:CONV_PAD, :] = aext[tm:tm + CONV_PAD, :]
    for rs, x2, acc in zip(subs, x2s, accs):
        x3 = x2 + acc
        if final_norm:
            x3 = _rms(x3, gfin_ref[...])
        y_ref[0, rs, :] = x3


def _memffn(x, o_pool, o_fox, mk, mv, conv_hist, w_o, g_xq, w_mq, w_mo, g_ffn, w_up, conv_w, conv_b,
            w_down, g_final, tm, final_norm):
    bsz, t, _ = x.shape
    blk = lambda b, i: (b, i, 0)
    full = lambda b, i: (b, 0, 0)
    kern = functools.partial(_memffn_kernel, tm=tm, sub=min(tm, 256), final_norm=final_norm)
    return pl.pallas_call(
        kern,
        grid=(bsz, t // tm),
        in_specs=[
            pl.BlockSpec((1, tm, D_MODEL), blk),
            pl.BlockSpec((1, tm, D_POOL), blk),
            pl.BlockSpec((1, tm, D_FOX), blk),
            pl.BlockSpec((1, N_MEM, D_MODEL), full),
            pl.BlockSpec((1, N_MEM, D_MODEL), full),
            pl.BlockSpec((1, CONV_W - 1, D_FF), full),
            _const_spec((D_MODEL, D_MODEL)),
            _const_spec((1, D_MODEL)),
            _const_spec((D_MODEL, D_MODEL)),
            _const_spec((D_MODEL, D_MODEL)),
            _const_spec((1, D_MODEL)),
            _const_spec((D_MODEL, 2 * D_FF)),
            _const_spec((CONV_W, D_FF)),
            _const_spec((1, D_FF)),
            _const_spec((D_FF, D_MODEL)),
            _const_spec((1, D_MODEL)),
        ],
        out_specs=[
            pl.BlockSpec((1, tm, D_MODEL), blk),
            pl.BlockSpec((1, CONV_W - 1, D_FF), full),
        ],
        out_shape=[
            jax.ShapeDtypeStruct((bsz, t, D_MODEL), F32),
            jax.ShapeDtypeStruct((bsz, CONV_W - 1, D_FF), F32),
        ],
        scratch_shapes=[
            pltpu.VMEM((CONV_PAD + tm, D_FF), F32),
            pltpu.VMEM((tm, D_MODEL), BF16),
        ],
        compiler_params=_params(("arbitrary", "arbitrary")),
        name="memffn",
    )(x, o_pool, o_fox, mk, mv, conv_hist, w_o, g_xq, w_mq, w_mo, g_ffn, w_up, conv_w, conv_b, w_down,
      g_final)


def _row_tile(n, target):
    tm = min(n, target)
    assert n % tm == 0
    return tm


def kernel(x_prompt, x_sample, cache_fox_k, cache_fox_v, cache_fox_logf, state_pool, state_ffn_conv, cache_mem_k, cache_mem_v, mem_prompt, g_mix, w_in, b_f, w_pool, pool_scale, w_o, g_xq, g_mkv, w_mq, w_mk, w_mv, w_mo, g_ffn, w_up, conv_w, conv_b, w_down, g_final):
    depth = w_in.shape[0]
    bp, tp, _ = x_prompt.shape
    bs, ts, _ = x_sample.shape
    past = cache_fox_k.shape[2]
    n_main = D_POOL + 3 * D_FOX

    xp, xs = x_prompt, x_sample
    outs = {name: [] for name in ("pk", "pv", "plf", "ppool", "pconv", "pmk", "pmv",
                                   "sk", "sv", "slf", "spool", "sconv")}
    for l in range(depth):
        last = l == depth - 1
        row2 = lambda a: a.reshape(1, -1).astype(F32)
        w_main = w_in[l][:, :n_main].astype(BF16)
        w_f = jnp.pad(w_in[l][:, n_main:], ((0, 0), (0, LANES - FOX_HEADS))).astype(BF16)
        bf = jnp.pad(b_f[l].astype(F32), (0, LANES - FOX_HEADS)).reshape(1, LANES)
        wp = w_pool[l].astype(BF16)
        ps = row2(pool_scale[l])
        ffn_w = (w_o[l].astype(BF16), row2(g_xq[l]), w_mq[l].astype(BF16), w_mo[l].astype(BF16),
                 row2(g_ffn[l]), w_up[l].astype(BF16), conv_w[l].astype(F32), row2(conv_b[l]),
                 w_down[l].astype(BF16), row2(g_final))

        mk, mv = _memkv(mem_prompt.reshape(bp * N_MEM, D_MODEL), row2(g_mkv[l]),
                        w_mk[l].astype(BF16), w_mv[l].astype(BF16), _row_tile(bp * N_MEM, 512))
        mk = mk.reshape(bp, N_MEM, D_MODEL)
        mv = mv.reshape(bp, N_MEM, D_MODEL)
        o_pool, u_tail, q, k_t, v_t, lf_t, kb_t, vb_t = _inproj(
            xp.reshape(bp * tp, D_MODEL), row2(g_mix[l]), w_main, w_f, bf, wp, ps, _row_tile(tp, 1024), tp,
            pool=True)
        o_fox = _fox_prompt(q.reshape(bp, tp, D_FOX), kb_t, vb_t, lf_t, _row_tile(tp, 256))
        xp, pconv = _memffn(xp, o_pool.reshape(bp, tp, D_POOL), o_fox, mk, mv,
                            jnp.zeros((bp, CONV_W - 1, D_FF), F32), *ffn_w,
                            tm=_row_tile(tp, 512), final_norm=last)
        outs["pk"].append(jnp.transpose(k_t, (0, 3, 1, 2)))
        outs["pv"].append(jnp.transpose(v_t, (0, 3, 1, 2)))
        outs["plf"].append(jnp.transpose(lf_t, (0, 2, 1)))
        outs["ppool"].append(u_tail[:, POOL_CARRY - POOL_HIST:, :])
        outs["pconv"].append(pconv)
        outs["pmk"].append(mk.reshape(bp, N_MEM, MEM_HEADS, MEM_HEAD_DIM))
        outs["pmv"].append(mv.reshape(bp, N_MEM, MEM_HEADS, MEM_HEAD_DIM))

        u, q, k, v, lf = _inproj(xs.reshape(bs * ts, D_MODEL), row2(g_mix[l]), w_main, w_f, bf,
                                 wp, ps, _row_tile(bs * ts, 512), ts, pool=False)
        u = u.reshape(bs, ts, D_POOL)
        k = k.reshape(bs, ts, D_FOX)
        v = v.reshape(bs, ts, D_FOX)
        lf = lf.reshape(bs, ts, FOX_HEADS)
        o_pool, o_fox = _mix_sample(u, q.reshape(bs, ts, D_FOX), k, v, lf,
                                    jnp.transpose(cache_fox_k[l], (0, 2, 3, 1)),
                                    jnp.transpose(cache_fox_v[l], (0, 2, 3, 1)),
                                    jnp.transpose(cache_fox_logf[l].astype(F32), (0, 2, 1)),
                                    state_pool[l], wp, ps, tc=512)
        xs, sconv = _memffn(xs, o_pool, o_fox, cache_mem_k[l].reshape(bs, N_MEM, D_MODEL),
                            cache_mem_v[l].reshape(bs, N_MEM, D_MODEL), state_ffn_conv[l], *ffn_w,
                            tm=ts, final_norm=last)
        outs["sk"].append(k.reshape(bs, ts, FOX_HEADS, FOX_HEAD_DIM))
        outs["sv"].append(v.reshape(bs, ts, FOX_HEADS, FOX_HEAD_DIM))
        outs["slf"].append(lf)
        outs["spool"].append(u[:, ts - POOL_HIST:, :])
        outs["sconv"].append(sconv)

    st = {name: jnp.stack(vals) for name, vals in outs.items()}
    return (xp, xs, st["pk"], st["pv"], st["plf"], st["ppool"], st["pconv"], st["pmk"], st["pmv"],
            st["sk"], st["sv"], st["slf"], st["spool"], st["sconv"])
```

```python
import functools
import math

import jax
import jax.numpy as jnp
from jax import lax
from jax.experimental import pallas as pl
from jax.experimental.pallas import tpu as pltpu

D_MODEL = 1024
D_POOL = 512
POOL_WINDOWS = (2, 4, 8, 16)
POOL_GROUP_W = 128
POOL_HIST = 15
POOL_CARRY = 16
POOL_BASE = 24
D_FOX = 512
FOX_HEADS = 8
FOX_HEAD_DIM = 64
FOX_PAIRS = FOX_HEADS // 2
N_MEM = 256
MEM_HEADS = 4
MEM_HEAD_DIM = 256
D_FF = 2816
CONV_W = 3
CONV_PAD = 8
FF_CHUNKS = ((0, 1024), (1024, 2048), (2048, 2816))
EPS = 1e-6
LOG2E = math.log2(math.e)
LANES = 128
VMEM_LIMIT = 60 * 1024 * 1024

F32 = jnp.float32
BF16 = jnp.bfloat16


def _dot(a, b):
    return jnp.dot(a, b, preferred_element_type=F32)


def _dot_nt(a, b):
    return lax.dot_general(a, b, (((1,), (1,)), ((), ())), preferred_element_type=F32)


def _rms(x, g):
    ms = jnp.mean(x * x, axis=-1, keepdims=True)
    return (x * lax.rsqrt(ms + EPS)) * g


def _const_spec(shape):
    nd = len(shape)
    return pl.BlockSpec(shape, lambda *_: (0,) * nd, pipeline_mode=pl.Buffered(1))


def _params(sem):
    return pltpu.CompilerParams(dimension_semantics=sem, vmem_limit_bytes=VMEM_LIMIT)


def _tril(n):
    r = lax.broadcasted_iota(jnp.int32, (n, n), 0)
    c = lax.broadcasted_iota(jnp.int32, (n, n), 1)
    return jnp.where(r >= c, 1.0, 0.0).astype(BF16)


def _triu(n):
    r = lax.broadcasted_iota(jnp.int32, (n, n), 0)
    c = lax.broadcasted_iota(jnp.int32, (n, n), 1)
    return jnp.where(r <= c, 1.0, 0.0).astype(BF16)


def _split3(x):
    hi = x.astype(BF16)
    r1 = x - hi.astype(F32)
    mid = r1.astype(BF16)
    lo = (r1 - mid.astype(F32)).astype(BF16)
    return hi, mid, lo


def _cumsum_rows(x, tril, carry):
    hi, mid, lo = _split3(x)
    return _dot(tril, hi) + _dot(tril, mid) + _dot(tril, lo) + carry


def _cumsum_lanes(x, triu, carry):
    hi, mid, lo = _split3(x)
    return _dot(hi, triu) + _dot(mid, triu) + _dot(lo, triu) + carry


def _pool_mixer(uext_ref, tmp_a, tmp_b, tq, t0, pos0, wpool_ref, pscale_ref, out_ref):
    lo, hi = 8, POOL_BASE + tq
    t_idx = t0 + lax.broadcasted_iota(jnp.int32, (tq, 1), 0)
    for g, w in enumerate(POOL_WINDOWS):
        sl = slice(g * POOL_GROUP_W, (g + 1) * POOL_GROUP_W)
        src, bufs, d = uext_ref, (tmp_a, tmp_b), 1
        col = sl
        while 2 * d < w:
            dst = bufs[0]
            dst[lo:hi, :] = src[lo:hi, col] + src[lo - d:hi - d, col]
            src, bufs, d, col = dst, (bufs[1], bufs[0]), 2 * d, slice(0, POOL_GROUP_W)
        s = src[POOL_BASE:hi, col] + src[POOL_BASE - d:hi - d, col]
        u = uext_ref[POOL_BASE:hi, sl]
        count = jnp.minimum(pos0 + t_idx + 1, w).astype(F32)
        diff = (s / count - u).astype(BF16)
        y = _dot(diff, wpool_ref[g]) * pscale_ref[:, sl]
        out_ref[:, sl] = y.astype(out_ref.dtype)


def _fold_lanes(x, op):
    parts = [x[:, c:c + LANES] for c in range(0, x.shape[1], LANES)]
    return functools.reduce(op, parts)


def _inproj_kernel(x_ref, g_ref, w_ref, wf_ref, bf_ref, wpool_ref, pscale_ref, *refs, tm, sub, bps, pool):
    if pool:
        op_ref, ut_ref, q_ref, k_ref, v_ref, lf_ref, kb_ref, vb_ref, uext, tmp_a, tmp_b = refs
    else:
        u_ref, q_ref, k_ref, v_ref, lf_ref = refs
    if pool:
        blk = pl.program_id(0) % bps

        @pl.when(blk == 0)
        def _no_history():
            uext[0:POOL_BASE, :] = jnp.zeros((POOL_BASE, D_POOL), F32)
            tmp_a[0:8, :] = jnp.zeros((8, POOL_GROUP_W), F32)
            tmp_b[0:8, :] = jnp.zeros((8, POOL_GROUP_W), F32)

    subs = [slice(r0, r0 + sub) for r0 in range(0, tm, sub)]
    hs = [_rms(x_ref[rs, :], g_ref[...]).astype(BF16) for rs in subs]
    for rs, h in zip(subs, hs):
        u = _dot(h, w_ref[:, 0:D_POOL])
        if pool:
            uext[POOL_BASE + rs.start:POOL_BASE + rs.stop, :] = u
        else:
            u_ref[rs, :] = u
    if pool:
        _pool_mixer(uext, tmp_a, tmp_b, tm, blk * tm, 0, wpool_ref, pscale_ref, op_ref)
        tail = uext[tm + POOL_BASE - POOL_CARRY:tm + POOL_BASE, :]
        ut_ref[0] = tail
        uext[POOL_BASE - POOL_CARRY:POOL_BASE, :] = tail
    for rs, h in zip(subs, hs):
        q_ref[rs, :] = (_dot(h, w_ref[:, D_POOL:D_POOL + D_FOX]) * (FOX_HEAD_DIM ** -0.5 * LOG2E)).astype(BF16)
        k = _dot(h, w_ref[:, D_POOL + D_FOX:D_POOL + 2 * D_FOX])
        v = _dot(h, w_ref[:, D_POOL + 2 * D_FOX:D_POOL + 3 * D_FOX])
        zf = _dot(h, wf_ref[...]) + bf_ref[...]
        lf = -(jnp.maximum(-zf, 0.0) + jnp.log1p(jnp.exp(-jnp.abs(zf))))
        if pool:
            k_t = k.T.reshape(FOX_HEADS, FOX_HEAD_DIM, sub)
            v_t = v.T.reshape(FOX_HEADS, FOX_HEAD_DIM, sub)
            k_ref[0, :, :, rs] = k_t
            v_ref[0, :, :, rs] = v_t
            kb_ref[0, :, :, rs] = k_t.astype(BF16)
            vb_ref[0, :, :, rs] = v_t.astype(BF16)
            lf_ref[0, :, rs] = lf.T[0:FOX_HEADS, :]
        else:
            k_ref[rs, :] = k
            v_ref[rs, :] = v
            lf_ref[rs, :] = lf[:, 0:FOX_HEADS]


def _inproj(x2d, g, w_main, w_f, b_f, w_pool, pool_scale, tm, seq_len, pool):
    n = x2d.shape[0]
    bps = seq_len // tm
    row = lambda i: (i, 0)
    if pool:
        nb = n // seq_len
        kv_spec = pl.BlockSpec((1, FOX_HEADS, FOX_HEAD_DIM, tm), lambda i: (i // bps, 0, 0, i % bps))
        kv_shape = jax.ShapeDtypeStruct((nb, FOX_HEADS, FOX_HEAD_DIM, seq_len), F32)
        kvb_shape = jax.ShapeDtypeStruct((nb, FOX_HEADS, FOX_HEAD_DIM, seq_len), BF16)
        out_specs = [pl.BlockSpec((tm, D_POOL), row),
                     pl.BlockSpec((1, POOL_CARRY, D_POOL), lambda i: (i // bps, 0, 0)),
                     pl.BlockSpec((tm, D_FOX), row), kv_spec, kv_spec,
                     pl.BlockSpec((1, FOX_HEADS, tm), lambda i: (i // bps, 0, i % bps)),
                     kv_spec, kv_spec]
        out_shape = [jax.ShapeDtypeStruct((n, D_POOL), BF16),
                     jax.ShapeDtypeStruct((nb, POOL_CARRY, D_POOL), F32),
                     jax.ShapeDtypeStruct((n, D_FOX), BF16), kv_shape, kv_shape,
                     jax.ShapeDtypeStruct((nb, FOX_HEADS, seq_len), F32),
                     kvb_shape, kvb_shape]
        scratch = [pltpu.VMEM((POOL_BASE + tm, D_POOL), F32),
                   pltpu.VMEM((POOL_BASE + tm, POOL_GROUP_W), F32),
                   pltpu.VMEM((POOL_BASE + tm, POOL_GROUP_W), F32)]
    else:
        out_specs = [pl.BlockSpec((tm, D_POOL), row), pl.BlockSpec((tm, D_FOX), row),
                     pl.BlockSpec((tm, D_FOX), row), pl.BlockSpec((tm, D_FOX), row),
                     pl.BlockSpec((tm, FOX_HEADS), row)]
        out_shape = [jax.ShapeDtypeStruct((n, D_POOL), F32), jax.ShapeDtypeStruct((n, D_FOX), BF16),
                     jax.ShapeDtypeStruct((n, D_FOX), F32), jax.ShapeDtypeStruct((n, D_FOX), F32),
                     jax.ShapeDtypeStruct((n, FOX_HEADS), F32)]
        scratch = []
    return pl.pallas_call(
        functools.partial(_inproj_kernel, tm=tm, sub=min(tm, 512), bps=bps, pool=pool),
        grid=(n // tm,),
        in_specs=[
            pl.BlockSpec((tm, D_MODEL), row),
            _const_spec((1, D_MODEL)),
            _const_spec((D_MODEL, D_POOL + 3 * D_FOX)),
            _const_spec((D_MODEL, LANES)),
            _const_spec((1, LANES)),
            _const_spec((len(POOL_WINDOWS), POOL_GROUP_W, POOL_GROUP_W)),
            _const_spec((1, D_POOL)),
        ],
        out_specs=out_specs,
        out_shape=out_shape,
        scratch_shapes=scratch,
        compiler_params=_params(("arbitrary",)),
        name="inproj_pool" if pool else "inproj",
    )(x2d, g, w_main, w_f, b_f, w_pool, pool_scale)


def _memkv_kernel(m_ref, g_ref, wk_ref, wv_ref, mk_ref, mv_ref, mk4_ref, mv4_ref):
    h = _rms(m_ref[...], g_ref[...]).astype(BF16)
    mk = _dot(h, wk_ref[...])
    mv = _dot(h, wv_ref[...])
    mk_ref[...] = mk
    mv_ref[...] = mv
    for hd in range(MEM_HEADS):
        hs = slice(hd * MEM_HEAD_DIM, (hd + 1) * MEM_HEAD_DIM)
        mk4_ref[:, hd, :] = mk[:, hs]
        mv4_ref[:, hd, :] = mv[:, hs]


def _memkv(m2d, g, wk, wv, tm):
    n = m2d.shape[0]
    row = lambda i: (i, 0)
    row3 = lambda i: (i, 0, 0)
    return pl.pallas_call(
        _memkv_kernel,
        grid=(n // tm,),
        in_specs=[
            pl.BlockSpec((tm, D_MODEL), row),
            _const_spec((1, D_MODEL)),
            _const_spec((D_MODEL, D_MODEL)),
            _const_spec((D_MODEL, D_MODEL)),
        ],
        out_specs=[pl.BlockSpec((tm, D_MODEL), row), pl.BlockSpec((tm, D_MODEL), row),
                   pl.BlockSpec((tm, MEM_HEADS, MEM_HEAD_DIM), row3),
                   pl.BlockSpec((tm, MEM_HEADS, MEM_HEAD_DIM), row3)],
        out_shape=[jax.ShapeDtypeStruct((n, D_MODEL), F32)] * 2
        + [jax.ShapeDtypeStruct((n, MEM_HEADS, MEM_HEAD_DIM), F32)] * 2,
        compiler_params=_params(("arbitrary",)),
        name="memkv",
    )(m2d, g, wk, wv)


def _fox_prompt_kernel(q_ref, k_ref, v_ref, lf_ref, o_ref, ccol, crow, cpad, *, nq, tq):
    pair = pl.program_id(1)
    lane = lax.broadcasted_iota(jnp.int32, (1, LANES), 1)
    low = lane < FOX_HEAD_DIM

    @pl.when(pair == 0)
    def _cumsum():
        triu = _triu(tq)
        cpad[...] = jnp.zeros_like(cpad)
        carry = jnp.zeros((FOX_HEADS, 1), F32)
        for b in range(nq):
            c = _cumsum_lanes(lf_ref[0, :, b * tq:(b + 1) * tq], triu, carry)
            carry = c[:, tq - 1:tq]
            c2 = c * LOG2E
            crow[b] = c2
            cpad[0:FOX_HEADS, :] = c2
            ccol[b] = cpad[...].T

    r_idx = lax.broadcasted_iota(jnp.int32, (tq, tq), 0)
    c_idx = lax.broadcasted_iota(jnp.int32, (tq, tq), 1)
    causal = r_idx >= c_idx
    zero = jnp.zeros((), BF16)
    for i in reversed(range(nq)):
        keys = (i + 1) * tq
        qp = q_ref[0, i * tq:(i + 1) * tq, :]
        cblk = ccol[i]
        o_blk = None
        for odd in range(2):
            head = 2 * pair + odd
            q_h = jnp.where(low, zero, qp) if odd else jnp.where(low, qp, zero)
            v_own = v_ref[0, odd, :, 0:keys]
            v_pad = jnp.zeros_like(v_own)
            v_h = jnp.concatenate([v_pad, v_own] if odd else [v_own, v_pad], axis=0)
            cq = jnp.sum(jnp.where(lane == head, cblk, 0.0), axis=-1, keepdims=True)
            s_list = []
            for j in range(i + 1):
                k_j = k_ref[0, :, :, j * tq:(j + 1) * tq].reshape(LANES, tq)
                s = _dot(q_h, k_j) - crow[j, pl.ds(head, 1), :]
                if j == i:
                    s = jnp.where(causal, s, -jnp.inf)
                s_list.append(s)
            m = functools.reduce(jnp.maximum, [_fold_lanes(s, jnp.maximum) for s in s_list])
            m_row = jnp.max(m, axis=-1, keepdims=True) + cq
            r = cq - m_row
            p_list = [jnp.exp2(s + r) for s in s_list]
            l = functools.reduce(jnp.add, [_fold_lanes(p, jnp.add) for p in p_list])
            l = jnp.sum(l, axis=-1, keepdims=True)
            p_cat = jnp.concatenate([p.astype(BF16) for p in p_list], axis=1)
            o_h = _dot_nt(p_cat, v_h) / l
            o_blk = o_h if o_blk is None else o_blk + o_h
        o_ref[0, i * tq:(i + 1) * tq, :] = o_blk.astype(BF16)


def _fox_prompt(q, k_t, v_t, lf_t, tq):
    bsz, t, _ = q.shape
    nq = t // tq
    col = lambda b, p: (b, 0, p)
    kv_spec = pl.BlockSpec((1, 2, FOX_HEAD_DIM, t), lambda b, p: (b, p, 0, 0))
    return pl.pallas_call(
        functools.partial(_fox_prompt_kernel, nq=nq, tq=tq),
        grid=(bsz, FOX_PAIRS),
        in_specs=[
            pl.BlockSpec((1, t, LANES), col),
            kv_spec,
            kv_spec,
            pl.BlockSpec((1, FOX_HEADS, t), lambda b, p: (b, 0, 0)),
        ],
        out_specs=pl.BlockSpec((1, t, LANES), col),
        out_shape=jax.ShapeDtypeStruct((bsz, t, D_FOX), BF16),
        scratch_shapes=[
            pltpu.VMEM((nq, tq, LANES), F32),
            pltpu.VMEM((nq, FOX_HEADS, tq), F32),
            pltpu.VMEM((LANES, tq), F32),
        ],
        compiler_params=_params(("arbitrary", "arbitrary")),
        name="fox_prompt",
    )(q, k_t, v_t, lf_t)


def _mix_sample_kernel(u_ref, q_ref, k_ref, v_ref, lf_ref, kc_ref, vc_ref, lfc_ref, ph_ref,
                       wpool_ref, pscale_ref, op_ref, of_ref, cbuf, crow, uext, tmp_a, tmp_b,
                       *, past, t, tc):
    nblk = past // tc
    triu = _triu(tc)
    carry = jnp.zeros((FOX_HEADS, 1), F32)
    for b in range(nblk):
        c = _cumsum_lanes(lfc_ref[0, :, b * tc:(b + 1) * tc], triu, carry)
        carry = c[:, tc - 1:tc]
        crow[:, b * tc:(b + 1) * tc] = c * LOG2E
    cbuf[...] = jnp.zeros_like(cbuf)
    cbuf[0:t, 0:FOX_HEADS] = lf_ref[0]
    cnew = _cumsum_rows(cbuf[...], _tril(LANES), jnp.zeros((1, LANES), F32))
    cnew_row = (cnew.T[0:FOX_HEADS, 0:t] + carry) * LOG2E

    uext[0:POOL_BASE - POOL_HIST, :] = jnp.zeros((POOL_BASE - POOL_HIST, D_POOL), F32)
    tmp_a[0:8, :] = jnp.zeros((8, POOL_GROUP_W), F32)
    tmp_b[0:8, :] = jnp.zeros((8, POOL_GROUP_W), F32)
    uext[POOL_BASE - POOL_HIST:POOL_BASE, :] = ph_ref[0]
    uext[POOL_BASE:POOL_BASE + t, :] = u_ref[0]
    _pool_mixer(uext, tmp_a, tmp_b, t, 0, past, wpool_ref, pscale_ref, op_ref.at[0])

    r_idx = lax.broadcasted_iota(jnp.int32, (t, t), 0)
    c_idx = lax.broadcasted_iota(jnp.int32, (t, t), 1)
    causal = r_idx >= c_idx
    for h in range(FOX_HEADS):
        hs = slice(h * FOX_HEAD_DIM, (h + 1) * FOX_HEAD_DIM)
        q_h = q_ref[0, :, hs]
        cq = (cnew[0:t, h:h + 1] + carry[h:h + 1, :]) * LOG2E
        s_hist = _dot(q_h, kc_ref[0, h].astype(BF16)) - crow[h:h + 1, :]
        s_new = _dot_nt(q_h, k_ref[0, :, hs].astype(BF16)) - cnew_row[h:h + 1, :]
        s_new = jnp.where(causal, s_new, -jnp.inf)
        m_row = jnp.maximum(jnp.max(s_hist, axis=-1, keepdims=True),
                            jnp.max(s_new, axis=-1, keepdims=True)) + cq
        r = cq - m_row
        p_hist = jnp.exp2(s_hist + r)
        p_new = jnp.exp2(s_new + r)
        l = jnp.sum(p_hist, axis=-1, keepdims=True) + jnp.sum(p_new, axis=-1, keepdims=True)
        acc = (_dot_nt(p_hist.astype(BF16), vc_ref[0, h].astype(BF16))
               + _dot(p_new.astype(BF16), v_ref[0, :, hs].astype(BF16)))
        of_ref[0, :, hs] = (acc / l).astype(BF16)


def _mix_sample(u, q, k, v, lf, kc_t, vc_t, lfc_t, ph, w_pool, pool_scale, tc):
    bsz, t, _ = u.shape
    past = kc_t.shape[3]
    assert t <= LANES and past % tc == 0 and t >= POOL_HIST
    one = lambda b: (b, 0, 0)
    cache_spec = pl.BlockSpec((1, FOX_HEADS, FOX_HEAD_DIM, past), lambda b: (b, 0, 0, 0))
    kern = functools.partial(_mix_sample_kernel, past=past, t=t, tc=tc)
    return pl.pallas_call(
        kern,
        grid=(bsz,),
        in_specs=[
            pl.BlockSpec((1, t, D_POOL), one),
            pl.BlockSpec((1, t, D_FOX), one),
            pl.BlockSpec((1, t, D_FOX), one),
            pl.BlockSpec((1, t, D_FOX), one),
            pl.BlockSpec((1, t, FOX_HEADS), one),
            cache_spec,
            cache_spec,
            pl.BlockSpec((1, FOX_HEADS, past), one),
            pl.BlockSpec((1, POOL_HIST, D_POOL), one),
            _const_spec((len(POOL_WINDOWS), POOL_GROUP_W, POOL_GROUP_W)),
            _const_spec((1, D_POOL)),
        ],
        out_specs=[pl.BlockSpec((1, t, D_POOL), one), pl.BlockSpec((1, t, D_FOX), one)],
        out_shape=[jax.ShapeDtypeStruct((bsz, t, D_POOL), BF16),
                   jax.ShapeDtypeStruct((bsz, t, D_FOX), BF16)],
        scratch_shapes=[
            pltpu.VMEM((LANES, LANES), F32),
            pltpu.VMEM((FOX_HEADS, past), F32),
            pltpu.VMEM((POOL_BASE + t, D_POOL), F32),
            pltpu.VMEM((POOL_BASE + t, POOL_GROUP_W), F32),
            pltpu.VMEM((POOL_BASE + t, POOL_GROUP_W), F32),
        ],
        compiler_params=_params(("arbitrary",)),
        name="mix_sample",
    )(u, q, k, v, lf, kc_t, vc_t, lfc_t, ph, w_pool, pool_scale)


def _row_groups(nb, tm, sub):
    if tm >= sub:
        return [(b, b + 1, r0, r0 + sub) for b in range(nb) for r0 in range(0, tm, sub)]
    per = sub // tm
    return [(b0, b0 + per, 0, tm) for b0 in range(0, nb, per)]


def _memffn_kernel(x_ref, op_ref, of_ref, mk_ref, mv_ref, ch_ref, wo_ref, gxq_ref, wmq_ref, wmo_ref,
                   gffn_ref, wup_ref, cw_ref, cb_ref, wdown_ref, gfin_ref, y_ref, cs_ref, aext, mo_sc,
                   *, nb, tm, sub, final_norm):
    @pl.when(pl.program_id(1) == 0)
    def _hist():
        for b in range(nb):
            aext[b, CONV_PAD - (CONV_W - 1):CONV_PAD, :] = ch_ref[b]

    groups = _row_groups(nb, tm, sub)

    def load(ref, g):
        b0, b1, r0, r1 = g
        return ref[b0:b1, r0:r1, :].reshape(sub, ref.shape[-1])

    def pieces(g):
        b0, b1, r0, r1 = g
        n = r1 - r0
        return [(b, slice((b - b0) * n, (b - b0 + 1) * n), slice(r0, r1)) for b in range(b0, b1)]

    xs = [load(x_ref, g) + _dot(load(op_ref, g), wo_ref[0:D_POOL, :])
          + _dot(load(of_ref, g), wo_ref[D_POOL:D_MODEL, :]) for g in groups]

    mqs = [(_dot(_rms(x, gxq_ref[...]).astype(BF16), wmq_ref[...]) * (MEM_HEAD_DIM ** -0.5)).astype(BF16)
           for x in xs]
    for b in range(nb):
        for h in range(MEM_HEADS):
            hs = slice(h * MEM_HEAD_DIM, (h + 1) * MEM_HEAD_DIM)
            mk_h = mk_ref[b, :, hs].astype(BF16)
            mv_h = mv_ref[b, :, hs].astype(BF16)
            for gi, g in enumerate(groups):
                for pb, pr, _ in pieces(g):
                    if pb != b:
                        continue
                    s = _dot_nt(mqs[gi][pr, hs], mk_h)
                    m = jnp.max(s, axis=-1, keepdims=True)
                    p = jnp.exp(s - m)
                    l = jnp.sum(p, axis=-1, keepdims=True)
                    o = _dot(p.astype(BF16), mv_h) / l
                    mo_sc[gi * sub + pr.start:gi * sub + pr.stop, hs] = o.astype(BF16)
    x2s = [x + _dot(mo_sc[gi * sub:(gi + 1) * sub, :], wmo_ref[...]) for gi, x in enumerate(xs)]

    hfs = [_rms(x2, gffn_ref[...]).astype(BF16) for x2 in x2s]
    accs = [jnp.zeros((sub, D_MODEL), F32) for _ in groups]
    for c0, c1 in FF_CHUNKS:
        cs = slice(c0, c1)
        for gi, (g, hf) in enumerate(zip(groups, hfs)):
            a = _dot(hf, wup_ref[:, cs])
            b_up = _dot(hf, wup_ref[:, D_FF + c0:D_FF + c1])
            convs = []
            for pb, pr, sr in pieces(g):
                r0, r1 = CONV_PAD + sr.start, CONV_PAD + sr.stop
                aext[pb, r0:r1, cs] = a[pr, :]
                convs.append(aext[pb, r0 - 2:r1 - 2, cs] * cw_ref[0:1, cs]
                             + aext[pb, r0 - 1:r1 - 1, cs] * cw_ref[1:2, cs]
                             + a[pr, :] * cw_ref[2:3, cs])
            conv = convs[0] if len(convs) == 1 else jnp.concatenate(convs, axis=0)
            conv = cb_ref[:, cs] + conv
            gate = conv * (1.0 / (1.0 + jnp.exp(-conv)))
            accs[gi] = accs[gi] + _dot((gate * b_up).astype(BF16), wdown_ref[cs, :])
    for b in range(nb):
        cs_ref[b] = aext[b, CONV_PAD + tm - (CONV_W - 1):CONV_PAD + tm, :]
        aext[b, 0:CONV_PAD, :] = aext[b, tm:tm + CONV_PAD, :]
    for g, x2, acc in zip(groups, x2s, accs):
        b0, b1, r0, r1 = g
        x3 = x2 + acc
        if final_norm:
            x3 = _rms(x3, gfin_ref[...])
        y_ref[b0:b1, r0:r1, :] = x3.reshape(b1 - b0, r1 - r0, D_MODEL)


def _memffn(x, o_pool, o_fox, mk, mv, conv_hist, w_o, g_xq, w_mq, w_mo, g_ffn, w_up, conv_w, conv_b,
            w_down, g_final, nb, tm, final_norm):
    bsz, t, _ = x.shape
    sub = min(nb * tm, 256)
    assert bsz % nb == 0 and t % tm == 0 and (nb * tm) % sub == 0 and (tm % sub == 0 or sub % tm == 0)
    blk = lambda b, i: (b, i, 0)
    full = lambda b, i: (b, 0, 0)
    kern = functools.partial(_memffn_kernel, nb=nb, tm=tm, sub=sub, final_norm=final_norm)
    return pl.pallas_call(
        kern,
        grid=(bsz // nb, t // tm),
        in_specs=[
            pl.BlockSpec((nb, tm, D_MODEL), blk),
            pl.BlockSpec((nb, tm, D_POOL), blk),
            pl.BlockSpec((nb, tm, D_FOX), blk),
            pl.BlockSpec((nb, N_MEM, D_MODEL), full),
            pl.BlockSpec((nb, N_MEM, D_MODEL), full),
            pl.BlockSpec((nb, CONV_W - 1, D_FF), full),
            _const_spec((D_MODEL, D_MODEL)),
            _const_spec((1, D_MODEL)),
            _const_spec((D_MODEL, D_MODEL)),
            _const_spec((D_MODEL, D_MODEL)),
            _const_spec((1, D_MODEL)),
            _const_spec((D_MODEL, 2 * D_FF)),
            _const_spec((CONV_W, D_FF)),
            _const_spec((1, D_FF)),
            _const_spec((D_FF, D_MODEL)),
            _const_spec((1, D_MODEL)),
        ],
        out_specs=[
            pl.BlockSpec((nb, tm, D_MODEL), blk),
            pl.BlockSpec((nb, CONV_W - 1, D_FF), full),
        ],
        out_shape=[
            jax.ShapeDtypeStruct((bsz, t, D_MODEL), F32),
            jax.ShapeDtypeStruct((bsz, CONV_W - 1, D_FF), F32),
        ],
        scratch_shapes=[
            pltpu.VMEM((nb, CONV_PAD + tm, D_FF), F32),
            pltpu.VMEM((nb * tm, D_MODEL), BF16),
        ],
        compiler_params=_params(("arbitrary", "arbitrary")),
        name="memffn",
    )(x, o_pool, o_fox, mk, mv, conv_hist, w_o, g_xq, w_mq, w_mo, g_ffn, w_up, conv_w, conv_b, w_down,
      g_final)


def _row_tile(n, target):
    tm = min(n, target)
    assert n % tm == 0
    return tm


def kernel(x_prompt, x_sample, cache_fox_k, cache_fox_v, cache_fox_logf, state_pool, state_ffn_conv, cache_mem_k, cache_mem_v, mem_prompt, g_mix, w_in, b_f, w_pool, pool_scale, w_o, g_xq, g_mkv, w_mq, w_mk, w_mv, w_mo, g_ffn, w_up, conv_w, conv_b, w_down, g_final):
    depth = w_in.shape[0]
    bp, tp, _ = x_prompt.shape
    bs, ts, _ = x_sample.shape
    past = cache_fox_k.shape[2]
    n_main = D_POOL + 3 * D_FOX

    xp, xs = x_prompt, x_sample
    outs = {name: [] for name in ("pk", "pv", "plf", "ppool", "pconv", "pmk", "pmv",
                                   "sk", "sv", "slf", "spool", "sconv")}
    for l in range(depth):
        last = l == depth - 1
        row2 = lambda a: a.reshape(1, -1).astype(F32)
        w_main = w_in[l][:, :n_main].astype(BF16)
        w_f = jnp.pad(w_in[l][:, n_main:], ((0, 0), (0, LANES - FOX_HEADS))).astype(BF16)
        bf = jnp.pad(b_f[l].astype(F32), (0, LANES - FOX_HEADS)).reshape(1, LANES)
        wp = w_pool[l].astype(BF16)
        ps = row2(pool_scale[l])
        ffn_w = (w_o[l].astype(BF16), row2(g_xq[l]), w_mq[l].astype(BF16), w_mo[l].astype(BF16),
                 row2(g_ffn[l]), w_up[l].astype(BF16), conv_w[l].astype(F32), row2(conv_b[l]),
                 w_down[l].astype(BF16), row2(g_final))

        mk, mv, mk4, mv4 = _memkv(mem_prompt.reshape(bp * N_MEM, D_MODEL), row2(g_mkv[l]),
                                  w_mk[l].astype(BF16), w_mv[l].astype(BF16), _row_tile(bp * N_MEM, 512))
        mk = mk.reshape(bp, N_MEM, D_MODEL)
        mv = mv.reshape(bp, N_MEM, D_MODEL)
        o_pool, u_tail, q, k_t, v_t, lf_t, kb_t, vb_t = _inproj(
            xp.reshape(bp * tp, D_MODEL), row2(g_mix[l]), w_main, w_f, bf, wp, ps, _row_tile(tp, 1024), tp,
            pool=True)
        o_fox = _fox_prompt(q.reshape(bp, tp, D_FOX), kb_t, vb_t, lf_t, _row_tile(tp, 256))
        xp, pconv = _memffn(xp, o_pool.reshape(bp, tp, D_POOL), o_fox, mk, mv,
                            jnp.zeros((bp, CONV_W - 1, D_FF), F32), *ffn_w,
                            nb=1, tm=_row_tile(tp, 512), final_norm=last)
        outs["pk"].append(jnp.transpose(k_t, (0, 3, 1, 2)))
        outs["pv"].append(jnp.transpose(v_t, (0, 3, 1, 2)))
        outs["plf"].append(jnp.transpose(lf_t, (0, 2, 1)))
        outs["ppool"].append(u_tail[:, POOL_CARRY - POOL_HIST:, :])
        outs["pconv"].append(pconv)
        outs["pmk"].append(mk4.reshape(bp, N_MEM, MEM_HEADS, MEM_HEAD_DIM))
        outs["pmv"].append(mv4.reshape(bp, N_MEM, MEM_HEADS, MEM_HEAD_DIM))

        u, q, k, v, lf = _inproj(xs.reshape(bs * ts, D_MODEL), row2(g_mix[l]), w_main, w_f, bf,
                                 wp, ps, _row_tile(bs * ts, 512), ts, pool=False)
        u = u.reshape(bs, ts, D_POOL)
        k = k.reshape(bs, ts, D_FOX)
        v = v.reshape(bs, ts, D_FOX)
        lf = lf.reshape(bs, ts, FOX_HEADS)
        o_pool, o_fox = _mix_sample(u, q.reshape(bs, ts, D_FOX), k, v, lf,
                                    jnp.transpose(cache_fox_k[l], (0, 2, 3, 1)),
                                    jnp.transpose(cache_fox_v[l], (0, 2, 3, 1)),
                                    jnp.transpose(cache_fox_logf[l].astype(F32), (0, 2, 1)),
                                    state_pool[l], wp, ps, tc=512)
        xs, sconv = _memffn(xs, o_pool, o_fox, cache_mem_k[l].reshape(bs, N_MEM, D_MODEL),
                            cache_mem_v[l].reshape(bs, N_MEM, D_MODEL), state_ffn_conv[l], *ffn_w,
                            nb=_row_tile(bs, 4), tm=ts, final_norm=last)
        outs["sk"].append(k.reshape(bs, ts, FOX_HEADS, FOX_HEAD_DIM))
        outs["sv"].append(v.reshape(bs, ts, FOX_HEADS, FOX_HEAD_DIM))
        outs["slf"].append(lf)
        outs["spool"].append(u[:, ts - POOL_HIST:, :])
        outs["sconv"].append(sconv)

    st = {name: jnp.stack(vals) for name, vals in outs.items()}
    return (xp, xs, st["pk"], st["pv"], st["plf"], st["ppool"], st["pconv"], st["pmk"], st["pmv"],
            st["sk"], st["sv"], st["slf"], st["spool"], st["sconv"])
```

```python
import functools
import math

import jax
import jax.numpy as jnp
from jax import lax
from jax.experimental import pallas as pl
from jax.experimental.pallas import tpu as pltpu

D_MODEL = 1024
D_POOL = 512
POOL_WINDOWS = (2, 4, 8, 16)
POOL_GROUP_W = 128
POOL_HIST = 15
POOL_CARRY = 16
POOL_PAD = 8
POOL_BASE = POOL_PAD + POOL_CARRY
D_FOX = 512
FOX_HEADS = 8
FOX_HEAD_DIM = 64
FOX_PAIRS = FOX_HEADS // 2
N_MEM = 256
MEM_HEADS = 4
MEM_HEAD_DIM = 256
D_FF = 2816
CONV_W = 3
CONV_PAD = 8
EPS = 1e-6
LOG2E = math.log2(math.e)
LANES = 128
VMEM_LIMIT = 60 * 1024 * 1024

F32 = jnp.float32
BF16 = jnp.bfloat16


def _dot(a, b):
    return jnp.dot(a, b, preferred_element_type=F32)


def _dot_nt(a, b):
    return lax.dot_general(a, b, (((1,), (1,)), ((), ())), preferred_element_type=F32)


def _rms(x, g):
    ms = jnp.mean(x * x, axis=-1, keepdims=True)
    return (x * lax.rsqrt(ms + EPS)) * g


def _const_spec(shape):
    nd = len(shape)
    return pl.BlockSpec(shape, lambda *_: (0,) * nd, pipeline_mode=pl.Buffered(1))


def _params(sem):
    return pltpu.CompilerParams(dimension_semantics=sem, vmem_limit_bytes=VMEM_LIMIT)


def _tril(n):
    r = lax.broadcasted_iota(jnp.int32, (n, n), 0)
    c = lax.broadcasted_iota(jnp.int32, (n, n), 1)
    return jnp.where(r >= c, 1.0, 0.0).astype(BF16)


def _triu(n):
    r = lax.broadcasted_iota(jnp.int32, (n, n), 0)
    c = lax.broadcasted_iota(jnp.int32, (n, n), 1)
    return jnp.where(r <= c, 1.0, 0.0).astype(BF16)


def _split3(x):
    hi = x.astype(BF16)
    r1 = x - hi.astype(F32)
    mid = r1.astype(BF16)
    lo = (r1 - mid.astype(F32)).astype(BF16)
    return hi, mid, lo


def _cumsum_rows(x, tril, carry):
    hi, mid, lo = _split3(x)
    return _dot(tril, hi) + _dot(tril, mid) + _dot(tril, lo) + carry


def _cumsum_lanes(x, triu, carry):
    hi, mid, lo = _split3(x)
    return _dot(hi, triu) + _dot(mid, triu) + _dot(lo, triu) + carry


def _pool_mixer(uext_ref, tmp_a, tmp_b, tq, t0, pos0, wpool_ref, pscale_ref, out_ref):
    lo, hi = POOL_PAD, POOL_BASE + tq
    t_idx = t0 + lax.broadcasted_iota(jnp.int32, (tq, 1), 0)
    for g, w in enumerate(POOL_WINDOWS):
        sl = slice(g * POOL_GROUP_W, (g + 1) * POOL_GROUP_W)
        src, bufs, d = uext_ref, (tmp_a, tmp_b), 1
        col = sl
        while 2 * d < w:
            dst = bufs[0]
            dst[lo:hi, :] = src[lo:hi, col] + src[lo - d:hi - d, col]
            src, bufs, d, col = dst, (bufs[1], bufs[0]), 2 * d, slice(0, POOL_GROUP_W)
        s = src[POOL_BASE:hi, col] + src[POOL_BASE - d:hi - d, col]
        u = uext_ref[POOL_BASE:hi, sl]
        count = jnp.minimum(pos0 + t_idx + 1, w).astype(F32)
        diff = (s / count - u).astype(BF16)
        y = _dot(diff, wpool_ref[g]) * pscale_ref[:, sl]
        out_ref[:, sl] = y.astype(out_ref.dtype)


def _fold_lanes(x, op):
    parts = [x[:, c:c + LANES] for c in range(0, x.shape[1], LANES)]
    return functools.reduce(op, parts)


def _inproj_kernel(x_ref, g_ref, w_ref, wf_ref, bf_ref, wpool_ref, pscale_ref, *refs, tm, sub, bps, pool):
    if pool:
        op_ref, ut_ref, q_ref, k_ref, v_ref, lf_ref, kb_ref, vb_ref, uext, tmp_a, tmp_b = refs
    else:
        u_ref, q_ref, k_ref, v_ref, lf_ref = refs
    if pool:
        blk = pl.program_id(0) % bps

        @pl.when(blk == 0)
        def _no_history():
            uext[0:POOL_BASE, :] = jnp.zeros((POOL_BASE, D_POOL), F32)
            tmp_a[0:POOL_PAD, :] = jnp.zeros((POOL_PAD, POOL_GROUP_W), F32)
            tmp_b[0:POOL_PAD, :] = jnp.zeros((POOL_PAD, POOL_GROUP_W), F32)

    subs = [slice(r0, r0 + sub) for r0 in range(0, tm, sub)]
    hs = [_rms(x_ref[rs, :], g_ref[...]).astype(BF16) for rs in subs]
    for rs, h in zip(subs, hs):
        u = _dot(h, w_ref[:, 0:D_POOL])
        if pool:
            uext[POOL_BASE + rs.start:POOL_BASE + rs.stop, :] = u
        else:
            u_ref[rs, :] = u
    if pool:
        _pool_mixer(uext, tmp_a, tmp_b, tm, blk * tm, 0, wpool_ref, pscale_ref, op_ref)
        tail = uext[tm + POOL_BASE - POOL_CARRY:tm + POOL_BASE, :]
        ut_ref[0] = tail
        uext[POOL_BASE - POOL_CARRY:POOL_BASE, :] = tail
    for rs, h in zip(subs, hs):
        qkv = _dot(h, w_ref[:, D_POOL:D_POOL + 3 * D_FOX])
        q_ref[rs, :] = (qkv[:, 0:D_FOX] * (FOX_HEAD_DIM ** -0.5 * LOG2E)).astype(BF16)
        k = qkv[:, D_FOX:2 * D_FOX]
        v = qkv[:, 2 * D_FOX:3 * D_FOX]
        zf = _dot(h, wf_ref[...]) + bf_ref[...]
        lf = -(jnp.maximum(-zf, 0.0) + jnp.log1p(jnp.exp(-jnp.abs(zf))))
        if pool:
            k_t = k.T.reshape(FOX_HEADS, FOX_HEAD_DIM, sub)
            v_t = v.T.reshape(FOX_HEADS, FOX_HEAD_DIM, sub)
            k_ref[0, :, :, rs] = k_t
            v_ref[0, :, :, rs] = v_t
            kb_ref[0, :, :, rs] = k_t.astype(BF16)
            vb_ref[0, :, :, rs] = v_t.astype(BF16)
            lf_ref[0, :, rs] = lf.T[0:FOX_HEADS, :]
        else:
            k_ref[rs, :] = k
            v_ref[rs, :] = v
            lf_ref[rs, :] = lf[:, 0:FOX_HEADS]


def _inproj(x2d, g, w_main, w_f, b_f, w_pool, pool_scale, tm, seq_len, pool):
    n = x2d.shape[0]
    bps = seq_len // tm
    row = lambda i: (i, 0)
    if pool:
        nb = n // seq_len
        kv_spec = pl.BlockSpec((1, FOX_HEADS, FOX_HEAD_DIM, tm), lambda i: (i // bps, 0, 0, i % bps))
        kv_shape = jax.ShapeDtypeStruct((nb, FOX_HEADS, FOX_HEAD_DIM, seq_len), F32)
        kvb_shape = jax.ShapeDtypeStruct((nb, FOX_HEADS, FOX_HEAD_DIM, seq_len), BF16)
        out_specs = [pl.BlockSpec((tm, D_POOL), row),
                     pl.BlockSpec((1, POOL_CARRY, D_POOL), lambda i: (i // bps, 0, 0)),
                     pl.BlockSpec((tm, D_FOX), row), kv_spec, kv_spec,
                     pl.BlockSpec((1, FOX_HEADS, tm), lambda i: (i // bps, 0, i % bps)),
                     kv_spec, kv_spec]
        out_shape = [jax.ShapeDtypeStruct((n, D_POOL), BF16),
                     jax.ShapeDtypeStruct((nb, POOL_CARRY, D_POOL), F32),
                     jax.ShapeDtypeStruct((n, D_FOX), BF16), kv_shape, kv_shape,
                     jax.ShapeDtypeStruct((nb, FOX_HEADS, seq_len), F32),
                     kvb_shape, kvb_shape]
        scratch = [pltpu.VMEM((POOL_BASE + tm, D_POOL), F32),
                   pltpu.VMEM((POOL_BASE + tm, POOL_GROUP_W), F32),
                   pltpu.VMEM((POOL_BASE + tm, POOL_GROUP_W), F32)]
    else:
        out_specs = [pl.BlockSpec((tm, D_POOL), row), pl.BlockSpec((tm, D_FOX), row),
                     pl.BlockSpec((tm, D_FOX), row), pl.BlockSpec((tm, D_FOX), row),
                     pl.BlockSpec((tm, FOX_HEADS), row)]
        out_shape = [jax.ShapeDtypeStruct((n, D_POOL), F32), jax.ShapeDtypeStruct((n, D_FOX), BF16),
                     jax.ShapeDtypeStruct((n, D_FOX), F32), jax.ShapeDtypeStruct((n, D_FOX), F32),
                     jax.ShapeDtypeStruct((n, FOX_HEADS), F32)]
        scratch = []
    return pl.pallas_call(
        functools.partial(_inproj_kernel, tm=tm, sub=min(tm, 512), bps=bps, pool=pool),
        grid=(n // tm,),
        in_specs=[
            pl.BlockSpec((tm, D_MODEL), row),
            _const_spec((1, D_MODEL)),
            _const_spec((D_MODEL, D_POOL + 3 * D_FOX)),
            _const_spec((D_MODEL, LANES)),
            _const_spec((1, LANES)),
            _const_spec((len(POOL_WINDOWS), POOL_GROUP_W, POOL_GROUP_W)),
            _const_spec((1, D_POOL)),
        ],
        out_specs=out_specs,
        out_shape=out_shape,
        scratch_shapes=scratch,
        compiler_params=_params(("arbitrary",)),
        name="inproj_pool" if pool else "inproj",
    )(x2d, g, w_main, w_f, b_f, w_pool, pool_scale)


def _memkv_kernel(m_ref, g_ref, wk_ref, wv_ref, mk_ref, mv_ref, mk4_ref, mv4_ref):
    h = _rms(m_ref[...], g_ref[...]).astype(BF16)
    mk = _dot(h, wk_ref[...])
    mv = _dot(h, wv_ref[...])
    mk_ref[...] = mk
    mv_ref[...] = mv
    for hd in range(MEM_HEADS):
        hs = slice(hd * MEM_HEAD_DIM, (hd + 1) * MEM_HEAD_DIM)
        mk4_ref[:, hd, :] = mk[:, hs]
        mv4_ref[:, hd, :] = mv[:, hs]


def _memkv(m2d, g, wk, wv, tm):
    n = m2d.shape[0]
    row = lambda i: (i, 0)
    row3 = lambda i: (i, 0, 0)
    return pl.pallas_call(
        _memkv_kernel,
        grid=(n // tm,),
        in_specs=[
            pl.BlockSpec((tm, D_MODEL), row),
            _const_spec((1, D_MODEL)),
            _const_spec((D_MODEL, D_MODEL)),
            _const_spec((D_MODEL, D_MODEL)),
        ],
        out_specs=[pl.BlockSpec((tm, D_MODEL), row), pl.BlockSpec((tm, D_MODEL), row),
                   pl.BlockSpec((tm, MEM_HEADS, MEM_HEAD_DIM), row3),
                   pl.BlockSpec((tm, MEM_HEADS, MEM_HEAD_DIM), row3)],
        out_shape=[jax.ShapeDtypeStruct((n, D_MODEL), F32)] * 2
        + [jax.ShapeDtypeStruct((n, MEM_HEADS, MEM_HEAD_DIM), F32)] * 2,
        compiler_params=_params(("arbitrary",)),
        name="memkv",
    )(m2d, g, wk, wv)


def _fox_prompt_kernel(q_ref, k_ref, v_ref, lf_ref, o_ref, ccol, crow, cpad, *, nq, tq):
    pair = pl.program_id(1)
    lane = lax.broadcasted_iota(jnp.int32, (1, LANES), 1)
    low = lane < FOX_HEAD_DIM

    @pl.when(pair == 0)
    def _cumsum():
        triu = _triu(tq)
        cpad[...] = jnp.zeros_like(cpad)
        carry = jnp.zeros((FOX_HEADS, 1), F32)
        for b in range(nq):
            c = _cumsum_lanes(lf_ref[0, :, b * tq:(b + 1) * tq], triu, carry)
            carry = c[:, tq - 1:tq]
            c2 = c * LOG2E
            crow[b] = c2
            cpad[0:FOX_HEADS, :] = c2
            ccol[b] = cpad[...].T

    r_idx = lax.broadcasted_iota(jnp.int32, (tq, tq), 0)
    c_idx = lax.broadcasted_iota(jnp.int32, (tq, tq), 1)
    causal = r_idx >= c_idx
    zero = jnp.zeros((), BF16)
    for i in reversed(range(nq)):
        keys = (i + 1) * tq
        qp = q_ref[0, i * tq:(i + 1) * tq, :]
        cblk = ccol[i]
        o_blk = None
        for odd in range(2):
            head = 2 * pair + odd
            q_h = jnp.where(low, zero, qp) if odd else jnp.where(low, qp, zero)
            v_own = v_ref[0, odd, :, 0:keys]
            v_pad = jnp.zeros_like(v_own)
            v_h = jnp.concatenate([v_pad, v_own] if odd else [v_own, v_pad], axis=0)
            cq = jnp.sum(jnp.where(lane == head, cblk, 0.0), axis=-1, keepdims=True)
            s_list = []
            for j in range(i + 1):
                k_j = k_ref[0, :, :, j * tq:(j + 1) * tq].reshape(LANES, tq)
                s = _dot(q_h, k_j) - crow[j, pl.ds(head, 1), :]
                if j == i:
                    s = jnp.where(causal, s, -jnp.inf)
                s_list.append(s)
            m = functools.reduce(jnp.maximum, [_fold_lanes(s, jnp.maximum) for s in s_list])
            m_row = jnp.max(m, axis=-1, keepdims=True) + cq
            r = cq - m_row
            p_list = [jnp.exp2(s + r) for s in s_list]
            l = functools.reduce(jnp.add, [_fold_lanes(p, jnp.add) for p in p_list])
            l = jnp.sum(l, axis=-1, keepdims=True)
            p_cat = jnp.concatenate([p.astype(BF16) for p in p_list], axis=1)
            o_h = _dot_nt(p_cat, v_h) / l
            o_blk = o_h if o_blk is None else o_blk + o_h
        o_ref[0, i * tq:(i + 1) * tq, :] = o_blk.astype(BF16)


def _fox_prompt(q, k_t, v_t, lf_t, tq):
    bsz, t, _ = q.shape
    nq = t // tq
    col = lambda b, p: (b, 0, p)
    kv_spec = pl.BlockSpec((1, 2, FOX_HEAD_DIM, t), lambda b, p: (b, p, 0, 0))
    return pl.pallas_call(
        functools.partial(_fox_prompt_kernel, nq=nq, tq=tq),
        grid=(bsz, FOX_PAIRS),
        in_specs=[
            pl.BlockSpec((1, t, LANES), col),
            kv_spec,
            kv_spec,
            pl.BlockSpec((1, FOX_HEADS, t), lambda b, p: (b, 0, 0)),
        ],
        out_specs=pl.BlockSpec((1, t, LANES), col),
        out_shape=jax.ShapeDtypeStruct((bsz, t, D_FOX), BF16),
        scratch_shapes=[
            pltpu.VMEM((nq, tq, LANES), F32),
            pltpu.VMEM((nq, FOX_HEADS, tq), F32),
            pltpu.VMEM((LANES, tq), F32),
        ],
        compiler_params=_params(("arbitrary", "arbitrary")),
        name="fox_prompt",
    )(q, k_t, v_t, lf_t)


def _mix_sample_kernel(u_ref, q_ref, k_ref, v_ref, lf_ref, kc_ref, vc_ref, lfc_ref, ph_ref,
                       wpool_ref, pscale_ref, op_ref, of_ref, cbuf, crow, uext, tmp_a, tmp_b,
                       *, past, t, tc):
    nblk = past // tc
    triu = _triu(tc)
    carry = jnp.zeros((FOX_HEADS, 1), F32)
    for b in range(nblk):
        c = _cumsum_lanes(lfc_ref[0, :, b * tc:(b + 1) * tc], triu, carry)
        carry = c[:, tc - 1:tc]
        crow[:, b * tc:(b + 1) * tc] = c * LOG2E
    cbuf[...] = jnp.zeros_like(cbuf)
    cbuf[0:t, 0:FOX_HEADS] = lf_ref[0]
    cnew = _cumsum_rows(cbuf[...], _tril(LANES), jnp.zeros((1, LANES), F32))
    cnew_row = (cnew.T[0:FOX_HEADS, 0:t] + carry) * LOG2E

    uext[0:POOL_BASE - POOL_HIST, :] = jnp.zeros((POOL_BASE - POOL_HIST, D_POOL), F32)
    tmp_a[0:POOL_PAD, :] = jnp.zeros((POOL_PAD, POOL_GROUP_W), F32)
    tmp_b[0:POOL_PAD, :] = jnp.zeros((POOL_PAD, POOL_GROUP_W), F32)
    uext[POOL_BASE - POOL_HIST:POOL_BASE, :] = ph_ref[0]
    uext[POOL_BASE:POOL_BASE + t, :] = u_ref[0]
    _pool_mixer(uext, tmp_a, tmp_b, t, 0, past, wpool_ref, pscale_ref, op_ref.at[0])

    r_idx = lax.broadcasted_iota(jnp.int32, (t, t), 0)
    c_idx = lax.broadcasted_iota(jnp.int32, (t, t), 1)
    causal = r_idx >= c_idx
    for h in range(FOX_HEADS):
        hs = slice(h * FOX_HEAD_DIM, (h + 1) * FOX_HEAD_DIM)
        q_h = q_ref[0, :, hs]
        cq = (cnew[0:t, h:h + 1] + carry[h:h + 1, :]) * LOG2E
        s_hist = _dot(q_h, kc_ref[0, h].astype(BF16)) - crow[h:h + 1, :]
        s_new = _dot_nt(q_h, k_ref[0, :, hs].astype(BF16)) - cnew_row[h:h + 1, :]
        s_new = jnp.where(causal, s_new, -jnp.inf)
        m_row = jnp.maximum(jnp.max(s_hist, axis=-1, keepdims=True),
                            jnp.max(s_new, axis=-1, keepdims=True)) + cq
        r = cq - m_row
        p_hist = jnp.exp2(s_hist + r)
        p_new = jnp.exp2(s_new + r)
        l = jnp.sum(p_hist, axis=-1, keepdims=True) + jnp.sum(p_new, axis=-1, keepdims=True)
        acc = (_dot_nt(p_hist.astype(BF16), vc_ref[0, h].astype(BF16))
               + _dot(p_new.astype(BF16), v_ref[0, :, hs].astype(BF16)))
        of_ref[0, :, hs] = (acc / l).astype(BF16)


def _mix_sample(u, q, k, v, lf, kc_t, vc_t, lfc_t, ph, w_pool, pool_scale, tc):
    bsz, t, _ = u.shape
    past = kc_t.shape[3]
    assert t <= LANES and past % tc == 0 and t >= POOL_HIST
    one = lambda b: (b, 0, 0)
    cache_spec = pl.BlockSpec((1, FOX_HEADS, FOX_HEAD_DIM, past), lambda b: (b, 0, 0, 0))
    kern = functools.partial(_mix_sample_kernel, past=past, t=t, tc=tc)
    return pl.pallas_call(
        kern,
        grid=(bsz,),
        in_specs=[
            pl.BlockSpec((1, t, D_POOL), one),
            pl.BlockSpec((1, t, D_FOX), one),
            pl.BlockSpec((1, t, D_FOX), one),
            pl.BlockSpec((1, t, D_FOX), one),
            pl.BlockSpec((1, t, FOX_HEADS), one),
            cache_spec,
            cache_spec,
            pl.BlockSpec((1, FOX_HEADS, past), one),
            pl.BlockSpec((1, POOL_HIST, D_POOL), one),
            _const_spec((len(POOL_WINDOWS), POOL_GROUP_W, POOL_GROUP_W)),
            _const_spec((1, D_POOL)),
        ],
        out_specs=[pl.BlockSpec((1, t, D_POOL), one), pl.BlockSpec((1, t, D_FOX), one)],
        out_shape=[jax.ShapeDtypeStruct((bsz, t, D_POOL), BF16),
                   jax.ShapeDtypeStruct((bsz, t, D_FOX), BF16)],
        scratch_shapes=[
            pltpu.VMEM((LANES, LANES), F32),
            pltpu.VMEM((FOX_HEADS, past), F32),
            pltpu.VMEM((POOL_BASE + t, D_POOL), F32),
            pltpu.VMEM((POOL_BASE + t, POOL_GROUP_W), F32),
            pltpu.VMEM((POOL_BASE + t, POOL_GROUP_W), F32),
        ],
        compiler_params=_params(("arbitrary",)),
        name="mix_sample",
    )(u, q, k, v, lf, kc_t, vc_t, lfc_t, ph, w_pool, pool_scale)


def _row_groups(nb, tm, sub):
    if tm >= sub:
        return [(b, b + 1, r0, r0 + sub) for b in range(nb) for r0 in range(0, tm, sub)]
    per = sub // tm
    return [(b0, b0 + per, 0, tm) for b0 in range(0, nb, per)]


def _memffn_kernel(x_ref, op_ref, of_ref, mk_ref, mv_ref, ch_ref, wo_ref, gxq_ref, wmq_ref, wmo_ref,
                   gffn_ref, wup_ref, cw_ref, cb_ref, wdown_ref, gfin_ref, y_ref, cs_ref, aext, mo_sc,
                   *, nb, tm, sub, final_norm):
    @pl.when(pl.program_id(1) == 0)
    def _hist():
        for b in range(nb):
            aext[b, CONV_PAD - (CONV_W - 1):CONV_PAD, :] = ch_ref[b]

    groups = _row_groups(nb, tm, sub)

    def load(ref, g):
        b0, b1, r0, r1 = g
        return ref[b0:b1, r0:r1, :].reshape(sub, ref.shape[-1])

    def pieces(g):
        b0, b1, r0, r1 = g
        n = r1 - r0
        return [(b, slice((b - b0) * n, (b - b0 + 1) * n), slice(r0, r1)) for b in range(b0, b1)]

    xs = [load(x_ref, g) + _dot(jnp.concatenate([load(op_ref, g), load(of_ref, g)], axis=1), wo_ref[...])
          for g in groups]

    mqs = [(_dot(_rms(x, gxq_ref[...]).astype(BF16), wmq_ref[...]) * (MEM_HEAD_DIM ** -0.5)).astype(BF16)
           for x in xs]
    for b in range(nb):
        for h in range(MEM_HEADS):
            hs = slice(h * MEM_HEAD_DIM, (h + 1) * MEM_HEAD_DIM)
            mk_h = mk_ref[b, :, hs].astype(BF16)
            mv_h = mv_ref[b, :, hs].astype(BF16)
            for gi, g in enumerate(groups):
                for pb, pr, _ in pieces(g):
                    if pb != b:
                        continue
                    s = _dot_nt(mqs[gi][pr, hs], mk_h)
                    m = jnp.max(s, axis=-1, keepdims=True)
                    p = jnp.exp(s - m)
                    l = jnp.sum(p, axis=-1, keepdims=True)
                    o = _dot(p.astype(BF16), mv_h) / l
                    mo_sc[gi * sub + pr.start:gi * sub + pr.stop, hs] = o.astype(BF16)
    x2s = [x + _dot(mo_sc[gi * sub:(gi + 1) * sub, :], wmo_ref[...]) for gi, x in enumerate(xs)]

    hfs = [_rms(x2, gffn_ref[...]).astype(BF16) for x2 in x2s]
    downs = []
    for g, hf in zip(groups, hfs):
        a = _dot(hf, wup_ref[:, 0:D_FF])
        b_up = _dot(hf, wup_ref[:, D_FF:2 * D_FF])
        convs = []
        for pb, pr, sr in pieces(g):
            r0, r1 = CONV_PAD + sr.start, CONV_PAD + sr.stop
            aext[pb, r0:r1, :] = a[pr, :]
            convs.append(aext[pb, r0 - 2:r1 - 2, :] * cw_ref[0:1, :]
                         + aext[pb, r0 - 1:r1 - 1, :] * cw_ref[1:2, :]
                         + a[pr, :] * cw_ref[2:3, :])
        conv = convs[0] if len(convs) == 1 else jnp.concatenate(convs, axis=0)
        conv = cb_ref[...] + conv
        gate = conv * (1.0 / (1.0 + jnp.exp(-conv)))
        downs.append(_dot((gate * b_up).astype(BF16), wdown_ref[...]))
    for b in range(nb):
        cs_ref[b] = aext[b, CONV_PAD + tm - (CONV_W - 1):CONV_PAD + tm, :]
        aext[b, 0:CONV_PAD, :] = aext[b, tm:tm + CONV_PAD, :]
    for g, x2, down in zip(groups, x2s, downs):
        b0, b1, r0, r1 = g
        x3 = x2 + down
        if final_norm:
            x3 = _rms(x3, gfin_ref[...])
        y_ref[b0:b1, r0:r1, :] = x3.reshape(b1 - b0, r1 - r0, D_MODEL)


def _memffn(x, o_pool, o_fox, mk, mv, conv_hist, w_o, g_xq, w_mq, w_mo, g_ffn, w_up, conv_w, conv_b,
            w_down, g_final, nb, tm, final_norm):
    bsz, t, _ = x.shape
    sub = min(nb * tm, 256)
    assert bsz % nb == 0 and t % tm == 0 and (nb * tm) % sub == 0 and (tm % sub == 0 or sub % tm == 0)
    blk = lambda b, i: (b, i, 0)
    full = lambda b, i: (b, 0, 0)
    kern = functools.partial(_memffn_kernel, nb=nb, tm=tm, sub=sub, final_norm=final_norm)
    return pl.pallas_call(
        kern,
        grid=(bsz // nb, t // tm),
        in_specs=[
            pl.BlockSpec((nb, tm, D_MODEL), blk),
            pl.BlockSpec((nb, tm, D_POOL), blk),
            pl.BlockSpec((nb, tm, D_FOX), blk),
            pl.BlockSpec((nb, N_MEM, D_MODEL), full),
            pl.BlockSpec((nb, N_MEM, D_MODEL), full),
            pl.BlockSpec((nb, CONV_W - 1, D_FF), full),
            _const_spec((D_MODEL, D_MODEL)),
            _const_spec((1, D_MODEL)),
            _const_spec((D_MODEL, D_MODEL)),
            _const_spec((D_MODEL, D_MODEL)),
            _const_spec((1, D_MODEL)),
            _const_spec((D_MODEL, 2 * D_FF)),
            _const_spec((CONV_W, D_FF)),
            _const_spec((1, D_FF)),
            _const_spec((D_FF, D_MODEL)),
            _const_spec((1, D_MODEL)),
        ],
        out_specs=[
            pl.BlockSpec((nb, tm, D_MODEL), blk),
            pl.BlockSpec((nb, CONV_W - 1, D_FF), full),
        ],
        out_shape=[
            jax.ShapeDtypeStruct((bsz, t, D_MODEL), F32),
            jax.ShapeDtypeStruct((bsz, CONV_W - 1, D_FF), F32),
        ],
        scratch_shapes=[
            pltpu.VMEM((nb, CONV_PAD + tm, D_FF), F32),
            pltpu.VMEM((nb * tm, D_MODEL), BF16),
        ],
        compiler_params=_params(("arbitrary", "arbitrary")),
        name="memffn",
    )(x, o_pool, o_fox, mk, mv, conv_hist, w_o, g_xq, w_mq, w_mo, g_ffn, w_up, conv_w, conv_b, w_down,
      g_final)


def _row_tile(n, target):
    tm = min(n, target)
    assert n % tm == 0
    return tm


def kernel(x_prompt, x_sample, cache_fox_k, cache_fox_v, cache_fox_logf, state_pool, state_ffn_conv, cache_mem_k, cache_mem_v, mem_prompt, g_mix, w_in, b_f, w_pool, pool_scale, w_o, g_xq, g_mkv, w_mq, w_mk, w_mv, w_mo, g_ffn, w_up, conv_w, conv_b, w_down, g_final):
    depth = w_in.shape[0]
    bp, tp, _ = x_prompt.shape
    bs, ts, _ = x_sample.shape
    past = cache_fox_k.shape[2]
    n_main = D_POOL + 3 * D_FOX

    xp, xs = x_prompt, x_sample
    outs = {name: [] for name in ("pk", "pv", "plf", "ppool", "pconv", "pmk", "pmv",
                                   "sk", "sv", "slf", "spool", "sconv")}
    for l in range(depth):
        last = l == depth - 1
        row2 = lambda a: a.reshape(1, -1).astype(F32)
        w_main = w_in[l][:, :n_main].astype(BF16)
        w_f = jnp.pad(w_in[l][:, n_main:], ((0, 0), (0, LANES - FOX_HEADS))).astype(BF16)
        bf = jnp.pad(b_f[l].astype(F32), (0, LANES - FOX_HEADS)).reshape(1, LANES)
        wp = w_pool[l].astype(BF16)
        ps = row2(pool_scale[l])
        ffn_w = (w_o[l].astype(BF16), row2(g_xq[l]), w_mq[l].astype(BF16), w_mo[l].astype(BF16),
                 row2(g_ffn[l]), w_up[l].astype(BF16), conv_w[l].astype(F32), row2(conv_b[l]),
                 w_down[l].astype(BF16), row2(g_final))

        mk, mv, mk4, mv4 = _memkv(mem_prompt.reshape(bp * N_MEM, D_MODEL), row2(g_mkv[l]),
                                  w_mk[l].astype(BF16), w_mv[l].astype(BF16), _row_tile(bp * N_MEM, 512))
        mk = mk.reshape(bp, N_MEM, D_MODEL)
        mv = mv.reshape(bp, N_MEM, D_MODEL)
        o_pool, u_tail, q, k_t, v_t, lf_t, kb_t, vb_t = _inproj(
            xp.reshape(bp * tp, D_MODEL), row2(g_mix[l]), w_main, w_f, bf, wp, ps, _row_tile(tp, 1024), tp,
            pool=True)
        o_fox = _fox_prompt(q.reshape(bp, tp, D_FOX), kb_t, vb_t, lf_t, _row_tile(tp, 256))
        xp, pconv = _memffn(xp, o_pool.reshape(bp, tp, D_POOL), o_fox, mk, mv,
                            jnp.zeros((bp, CONV_W - 1, D_FF), F32), *ffn_w,
                            nb=1, tm=_row_tile(tp, 512), final_norm=last)
        outs["pk"].append(jnp.transpose(k_t, (0, 3, 1, 2)))
        outs["pv"].append(jnp.transpose(v_t, (0, 3, 1, 2)))
        outs["plf"].append(jnp.transpose(lf_t, (0, 2, 1)))
        outs["ppool"].append(u_tail[:, POOL_CARRY - POOL_HIST:, :])
        outs["pconv"].append(pconv)
        outs["pmk"].append(mk4.reshape(bp, N_MEM, MEM_HEADS, MEM_HEAD_DIM))
        outs["pmv"].append(mv4.reshape(bp, N_MEM, MEM_HEADS, MEM_HEAD_DIM))

        u, q, k, v, lf = _inproj(xs.reshape(bs * ts, D_MODEL), row2(g_mix[l]), w_main, w_f, bf,
                                 wp, ps, _row_tile(bs * ts, 512), ts, pool=False)
        u = u.reshape(bs, ts, D_POOL)
        k = k.reshape(bs, ts, D_FOX)
        v = v.reshape(bs, ts, D_FOX)
        lf = lf.reshape(bs, ts, FOX_HEADS)
        o_pool, o_fox = _mix_sample(u, q.reshape(bs, ts, D_FOX), k, v, lf,
                                    jnp.transpose(cache_fox_k[l], (0, 2, 3, 1)),
                                    jnp.transpose(cache_fox_v[l], (0, 2, 3, 1)),
                                    jnp.transpose(cache_fox_logf[l].astype(F32), (0, 2, 1)),
                                    state_pool[l], wp, ps, tc=512)
        xs, sconv = _memffn(xs, o_pool, o_fox, cache_mem_k[l].reshape(bs, N_MEM, D_MODEL),
                            cache_mem_v[l].reshape(bs, N_MEM, D_MODEL), state_ffn_conv[l], *ffn_w,
                            nb=_row_tile(bs, 4), tm=ts, final_norm=last)
        outs["sk"].append(k.reshape(bs, ts, FOX_HEADS, FOX_HEAD_DIM))
        outs["sv"].append(v.reshape(bs, ts, FOX_HEADS, FOX_HEAD_DIM))
        outs["slf"].append(lf)
        outs["spool"].append(u[:, ts - POOL_HIST:, :])
        outs["sconv"].append(sconv)

    st = {name: jnp.stack(vals) for name, vals in outs.items()}
    return (xp, xs, st["pk"], st["pv"], st["plf"], st["ppool"], st["pconv"], st["pmk"], st["pmv"],
            st["sk"], st["sv"], st["slf"], st["spool"], st["sconv"])
```

```python
import functools
import math

import jax
import jax.numpy as jnp
from jax import lax
from jax.experimental import pallas as pl
from jax.experimental.pallas import tpu as pltpu

D_MODEL = 1024
D_POOL = 512
POOL_WINDOWS = (2, 4, 8, 16)
POOL_GROUP_W = 128
POOL_HIST = 15
POOL_CARRY = 16
POOL_PAD = 8
POOL_BASE = POOL_PAD + POOL_CARRY
D_FOX = 512
FOX_HEADS = 8
FOX_HEAD_DIM = 64
FOX_PAIRS = FOX_HEADS // 2
N_MEM = 256
MEM_HEADS = 4
MEM_HEAD_DIM = 256
D_FF = 2816
CONV_W = 3
CONV_PAD = 8
EPS = 1e-6
LOG2E = math.log2(math.e)
LANES = 128
VMEM_LIMIT = 60 * 1024 * 1024

F32 = jnp.float32
BF16 = jnp.bfloat16


def _dot(a, b):
    return jnp.dot(a, b, preferred_element_type=F32)


def _dot_nt(a, b):
    return lax.dot_general(a, b, (((1,), (1,)), ((), ())), preferred_element_type=F32)


def _rms(x, g):
    ms = jnp.mean(x * x, axis=-1, keepdims=True)
    return (x * lax.rsqrt(ms + EPS)) * g


def _const_spec(shape):
    nd = len(shape)
    return pl.BlockSpec(shape, lambda *_: (0,) * nd, pipeline_mode=pl.Buffered(1))


def _params(sem):
    return pltpu.CompilerParams(dimension_semantics=sem, vmem_limit_bytes=VMEM_LIMIT)


def _tril(n):
    r = lax.broadcasted_iota(jnp.int32, (n, n), 0)
    c = lax.broadcasted_iota(jnp.int32, (n, n), 1)
    return jnp.where(r >= c, 1.0, 0.0).astype(BF16)


def _triu(n):
    r = lax.broadcasted_iota(jnp.int32, (n, n), 0)
    c = lax.broadcasted_iota(jnp.int32, (n, n), 1)
    return jnp.where(r <= c, 1.0, 0.0).astype(BF16)


def _split3(x):
    hi = x.astype(BF16)
    r1 = x - hi.astype(F32)
    mid = r1.astype(BF16)
    lo = (r1 - mid.astype(F32)).astype(BF16)
    return hi, mid, lo


def _cumsum_rows(x, tril, carry):
    hi, mid, lo = _split3(x)
    return _dot(tril, hi) + _dot(tril, mid) + _dot(tril, lo) + carry


def _cumsum_lanes(x, triu, carry):
    hi, mid, lo = _split3(x)
    return _dot(hi, triu) + _dot(mid, triu) + _dot(lo, triu) + carry


def _pool_mixer(uext_ref, tmp_a, tmp_b, tq, t0, pos0, wpool_ref, pscale_ref, out_ref):
    lo, hi = POOL_PAD, POOL_BASE + tq
    t_idx = t0 + lax.broadcasted_iota(jnp.int32, (tq, 1), 0)
    for g, w in enumerate(POOL_WINDOWS):
        sl = slice(g * POOL_GROUP_W, (g + 1) * POOL_GROUP_W)
        src, bufs, d = uext_ref, (tmp_a, tmp_b), 1
        col = sl
        while 2 * d < w:
            dst = bufs[0]
            dst[lo:hi, :] = src[lo:hi, col] + src[lo - d:hi - d, col]
            src, bufs, d, col = dst, (bufs[1], bufs[0]), 2 * d, slice(0, POOL_GROUP_W)
        s = src[POOL_BASE:hi, col] + src[POOL_BASE - d:hi - d, col]
        u = uext_ref[POOL_BASE:hi, sl]
        count = jnp.minimum(pos0 + t_idx + 1, w).astype(F32)
        diff = (s / count - u).astype(BF16)
        y = _dot(diff, wpool_ref[g]) * pscale_ref[:, sl]
        out_ref[:, sl] = y.astype(out_ref.dtype)


def _fold_lanes(x, op):
    parts = [x[:, c:c + LANES] for c in range(0, x.shape[1], LANES)]
    return functools.reduce(op, parts)


def _inproj_kernel(x_ref, g_ref, w_ref, wf_ref, bf_ref, wpool_ref, pscale_ref, *refs, tm, sub, bps, pool):
    if pool:
        op_ref, ut_ref, q_ref, k_ref, v_ref, lf_ref, uext, tmp_a, tmp_b = refs
    else:
        u_ref, q_ref, k_ref, v_ref, lf_ref = refs
    if pool:
        blk = pl.program_id(0) % bps

        @pl.when(blk == 0)
        def _no_history():
            uext[0:POOL_BASE, :] = jnp.zeros((POOL_BASE, D_POOL), F32)
            tmp_a[0:POOL_PAD, :] = jnp.zeros((POOL_PAD, POOL_GROUP_W), F32)
            tmp_b[0:POOL_PAD, :] = jnp.zeros((POOL_PAD, POOL_GROUP_W), F32)

    subs = [slice(r0, r0 + sub) for r0 in range(0, tm, sub)]
    hs = [_rms(x_ref[rs, :], g_ref[...]).astype(BF16) for rs in subs]
    for rs, h in zip(subs, hs):
        u = _dot(h, w_ref[:, 0:D_POOL])
        if pool:
            uext[POOL_BASE + rs.start:POOL_BASE + rs.stop, :] = u
        else:
            u_ref[rs, :] = u
    if pool:
        _pool_mixer(uext, tmp_a, tmp_b, tm, blk * tm, 0, wpool_ref, pscale_ref, op_ref)
        tail = uext[tm + POOL_BASE - POOL_CARRY:tm + POOL_BASE, :]
        ut_ref[0] = tail
        uext[POOL_BASE - POOL_CARRY:POOL_BASE, :] = tail
    for rs, h in zip(subs, hs):
        qkv = _dot(h, w_ref[:, D_POOL:D_POOL + 3 * D_FOX])
        q_ref[rs, :] = (qkv[:, 0:D_FOX] * (FOX_HEAD_DIM ** -0.5 * LOG2E)).astype(BF16)
        k = qkv[:, D_FOX:2 * D_FOX]
        v = qkv[:, 2 * D_FOX:3 * D_FOX]
        zf = _dot(h, wf_ref[...]) + bf_ref[...]
        lf = -(jnp.maximum(-zf, 0.0) + jnp.log1p(jnp.exp(-jnp.abs(zf))))
        if pool:
            k_t = k.T.reshape(FOX_HEADS, FOX_HEAD_DIM, sub)
            v_t = v.T.reshape(FOX_HEADS, FOX_HEAD_DIM, sub)
            k_ref[0, :, :, rs] = k_t
            v_ref[0, :, :, rs] = v_t
            lf_ref[0, :, rs] = lf.T[0:FOX_HEADS, :]
        else:
            k_ref[rs, :] = k
            v_ref[rs, :] = v
            lf_ref[rs, :] = lf[:, 0:FOX_HEADS]


def _inproj(x2d, g, w_main, w_f, b_f, w_pool, pool_scale, tm, seq_len, pool):
    n = x2d.shape[0]
    bps = seq_len // tm
    row = lambda i: (i, 0)
    if pool:
        nb = n // seq_len
        kv_spec = pl.BlockSpec((1, FOX_HEADS, FOX_HEAD_DIM, tm), lambda i: (i // bps, 0, 0, i % bps))
        kv_shape = jax.ShapeDtypeStruct((nb, FOX_HEADS, FOX_HEAD_DIM, seq_len), F32)
        out_specs = [pl.BlockSpec((tm, D_POOL), row),
                     pl.BlockSpec((1, POOL_CARRY, D_POOL), lambda i: (i // bps, 0, 0)),
                     pl.BlockSpec((tm, D_FOX), row), kv_spec, kv_spec,
                     pl.BlockSpec((1, FOX_HEADS, tm), lambda i: (i // bps, 0, i % bps))]
        out_shape = [jax.ShapeDtypeStruct((n, D_POOL), BF16),
                     jax.ShapeDtypeStruct((nb, POOL_CARRY, D_POOL), F32),
                     jax.ShapeDtypeStruct((n, D_FOX), BF16), kv_shape, kv_shape,
                     jax.ShapeDtypeStruct((nb, FOX_HEADS, seq_len), F32)]
        scratch = [pltpu.VMEM((POOL_BASE + tm, D_POOL), F32),
                   pltpu.VMEM((POOL_BASE + tm, POOL_GROUP_W), F32),
                   pltpu.VMEM((POOL_BASE + tm, POOL_GROUP_W), F32)]
    else:
        out_specs = [pl.BlockSpec((tm, D_POOL), row), pl.BlockSpec((tm, D_FOX), row),
                     pl.BlockSpec((tm, D_FOX), row), pl.BlockSpec((tm, D_FOX), row),
                     pl.BlockSpec((tm, FOX_HEADS), row)]
        out_shape = [jax.ShapeDtypeStruct((n, D_POOL), F32), jax.ShapeDtypeStruct((n, D_FOX), BF16),
                     jax.ShapeDtypeStruct((n, D_FOX), F32), jax.ShapeDtypeStruct((n, D_FOX), F32),
                     jax.ShapeDtypeStruct((n, FOX_HEADS), F32)]
        scratch = []
    return pl.pallas_call(
        functools.partial(_inproj_kernel, tm=tm, sub=min(tm, 512), bps=bps, pool=pool),
        grid=(n // tm,),
        in_specs=[
            pl.BlockSpec((tm, D_MODEL), row),
            _const_spec((1, D_MODEL)),
            _const_spec((D_MODEL, D_POOL + 3 * D_FOX)),
            _const_spec((D_MODEL, LANES)),
            _const_spec((1, LANES)),
            _const_spec((len(POOL_WINDOWS), POOL_GROUP_W, POOL_GROUP_W)),
            _const_spec((1, D_POOL)),
        ],
        out_specs=out_specs,
        out_shape=out_shape,
        scratch_shapes=scratch,
        compiler_params=_params(("arbitrary",)),
        name="inproj_pool" if pool else "inproj",
    )(x2d, g, w_main, w_f, b_f, w_pool, pool_scale)


def _memkv_kernel(m_ref, g_ref, wk_ref, wv_ref, mk_ref, mv_ref, mk4_ref, mv4_ref):
    h = _rms(m_ref[...], g_ref[...]).astype(BF16)
    mk = _dot(h, wk_ref[...])
    mv = _dot(h, wv_ref[...])
    mk_ref[...] = mk
    mv_ref[...] = mv
    for hd in range(MEM_HEADS):
        hs = slice(hd * MEM_HEAD_DIM, (hd + 1) * MEM_HEAD_DIM)
        mk4_ref[:, hd, :] = mk[:, hs]
        mv4_ref[:, hd, :] = mv[:, hs]


def _memkv(m2d, g, wk, wv, tm):
    n = m2d.shape[0]
    row = lambda i: (i, 0)
    row3 = lambda i: (i, 0, 0)
    return pl.pallas_call(
        _memkv_kernel,
        grid=(n // tm,),
        in_specs=[
            pl.BlockSpec((tm, D_MODEL), row),
            _const_spec((1, D_MODEL)),
            _const_spec((D_MODEL, D_MODEL)),
            _const_spec((D_MODEL, D_MODEL)),
        ],
        out_specs=[pl.BlockSpec((tm, D_MODEL), row), pl.BlockSpec((tm, D_MODEL), row),
                   pl.BlockSpec((tm, MEM_HEADS, MEM_HEAD_DIM), row3),
                   pl.BlockSpec((tm, MEM_HEADS, MEM_HEAD_DIM), row3)],
        out_shape=[jax.ShapeDtypeStruct((n, D_MODEL), F32)] * 2
        + [jax.ShapeDtypeStruct((n, MEM_HEADS, MEM_HEAD_DIM), F32)] * 2,
        compiler_params=_params(("arbitrary",)),
        name="memkv",
    )(m2d, g, wk, wv)


def _fox_prompt_kernel(q_ref, k_ref, v_ref, lf_ref, o_ref, ccol, crow, cpad, *, nq, tq):
    pair = pl.program_id(1)
    lane = lax.broadcasted_iota(jnp.int32, (1, LANES), 1)
    low = lane < FOX_HEAD_DIM

    @pl.when(pair == 0)
    def _cumsum():
        triu = _triu(tq)
        cpad[...] = jnp.zeros_like(cpad)
        carry = jnp.zeros((FOX_HEADS, 1), F32)
        for b in range(nq):
            c = _cumsum_lanes(lf_ref[0, :, b * tq:(b + 1) * tq], triu, carry)
            carry = c[:, tq - 1:tq]
            c2 = c * LOG2E
            crow[b] = c2
            cpad[0:FOX_HEADS, :] = c2
            ccol[b] = cpad[...].T

    r_idx = lax.broadcasted_iota(jnp.int32, (tq, tq), 0)
    c_idx = lax.broadcasted_iota(jnp.int32, (tq, tq), 1)
    causal = r_idx >= c_idx
    zero = jnp.zeros((), BF16)
    for i in reversed(range(nq)):
        keys = (i + 1) * tq
        qp = q_ref[0, i * tq:(i + 1) * tq, :]
        cblk = ccol[i]
        o_blk = None
        for odd in range(2):
            head = 2 * pair + odd
            q_h = jnp.where(low, zero, qp) if odd else jnp.where(low, qp, zero)
            v_own = v_ref[0, odd, :, 0:keys].astype(BF16)
            v_pad = jnp.zeros_like(v_own)
            v_h = jnp.concatenate([v_pad, v_own] if odd else [v_own, v_pad], axis=0)
            cq = jnp.sum(jnp.where(lane == head, cblk, 0.0), axis=-1, keepdims=True)
            s_list = []
            for j in range(i + 1):
                k_j = k_ref[0, :, :, j * tq:(j + 1) * tq].reshape(LANES, tq).astype(BF16)
                s = _dot(q_h, k_j) - crow[j, pl.ds(head, 1), :]
                if j == i:
                    s = jnp.where(causal, s, -jnp.inf)
                s_list.append(s)
            m = functools.reduce(jnp.maximum, [_fold_lanes(s, jnp.maximum) for s in s_list])
            m_row = jnp.max(m, axis=-1, keepdims=True) + cq
            r = cq - m_row
            p_list = [jnp.exp2(s + r) for s in s_list]
            l = functools.reduce(jnp.add, [_fold_lanes(p, jnp.add) for p in p_list])
            l = jnp.sum(l, axis=-1, keepdims=True)
            p_cat = jnp.concatenate([p.astype(BF16) for p in p_list], axis=1)
            o_h = _dot_nt(p_cat, v_h) / l
            o_blk = o_h if o_blk is None else o_blk + o_h
        o_ref[0, i * tq:(i + 1) * tq, :] = o_blk.astype(BF16)


def _fox_prompt(q, k_t, v_t, lf_t, tq):
    bsz, t, _ = q.shape
    nq = t // tq
    col = lambda b, p: (b, 0, p)
    kv_spec = pl.BlockSpec((1, 2, FOX_HEAD_DIM, t), lambda b, p: (b, p, 0, 0))
    return pl.pallas_call(
        functools.partial(_fox_prompt_kernel, nq=nq, tq=tq),
        grid=(bsz, FOX_PAIRS),
        in_specs=[
            pl.BlockSpec((1, t, LANES), col),
            kv_spec,
            kv_spec,
            pl.BlockSpec((1, FOX_HEADS, t), lambda b, p: (b, 0, 0)),
        ],
        out_specs=pl.BlockSpec((1, t, LANES), col),
        out_shape=jax.ShapeDtypeStruct((bsz, t, D_FOX), BF16),
        scratch_shapes=[
            pltpu.VMEM((nq, tq, LANES), F32),
            pltpu.VMEM((nq, FOX_HEADS, tq), F32),
            pltpu.VMEM((LANES, tq), F32),
        ],
        compiler_params=_params(("arbitrary", "arbitrary")),
        name="fox_prompt",
    )(q, k_t, v_t, lf_t)


def _mix_sample_kernel(u_ref, q_ref, k_ref, v_ref, lf_ref, kc_ref, vc_ref, lfc_ref, ph_ref,
                       wpool_ref, pscale_ref, op_ref, of_ref, cbuf, crow, uext, tmp_a, tmp_b,
                       *, past, t, tc):
    nblk = past // tc
    triu = _triu(tc)
    carry = jnp.zeros((FOX_HEADS, 1), F32)
    for b in range(nblk):
        c = _cumsum_lanes(lfc_ref[0, :, b * tc:(b + 1) * tc], triu, carry)
        carry = c[:, tc - 1:tc]
        crow[:, b * tc:(b + 1) * tc] = c * LOG2E
    cbuf[...] = jnp.zeros_like(cbuf)
    cbuf[0:t, 0:FOX_HEADS] = lf_ref[0]
    cnew = _cumsum_rows(cbuf[...], _tril(LANES), jnp.zeros((1, LANES), F32))
    cnew_row = (cnew.T[0:FOX_HEADS, 0:t] + carry) * LOG2E

    uext[0:POOL_BASE - POOL_HIST, :] = jnp.zeros((POOL_BASE - POOL_HIST, D_POOL), F32)
    tmp_a[0:POOL_PAD, :] = jnp.zeros((POOL_PAD, POOL_GROUP_W), F32)
    tmp_b[0:POOL_PAD, :] = jnp.zeros((POOL_PAD, POOL_GROUP_W), F32)
    uext[POOL_BASE - POOL_HIST:POOL_BASE, :] = ph_ref[0]
    uext[POOL_BASE:POOL_BASE + t, :] = u_ref[0]
    _pool_mixer(uext, tmp_a, tmp_b, t, 0, past, wpool_ref, pscale_ref, op_ref.at[0])

    r_idx = lax.broadcasted_iota(jnp.int32, (t, t), 0)
    c_idx = lax.broadcasted_iota(jnp.int32, (t, t), 1)
    causal = r_idx >= c_idx
    for h in range(FOX_HEADS):
        hs = slice(h * FOX_HEAD_DIM, (h + 1) * FOX_HEAD_DIM)
        q_h = q_ref[0, :, hs]
        cq = (cnew[0:t, h:h + 1] + carry[h:h + 1, :]) * LOG2E
        s_hist = _dot(q_h, kc_ref[0, h].astype(BF16)) - crow[h:h + 1, :]
        s_new = _dot_nt(q_h, k_ref[0, :, hs].astype(BF16)) - cnew_row[h:h + 1, :]
        s_new = jnp.where(causal, s_new, -jnp.inf)
        m_row = jnp.maximum(jnp.max(s_hist, axis=-1, keepdims=True),
                            jnp.max(s_new, axis=-1, keepdims=True)) + cq
        r = cq - m_row
        p_hist = jnp.exp2(s_hist + r)
        p_new = jnp.exp2(s_new + r)
        l = jnp.sum(p_hist, axis=-1, keepdims=True) + jnp.sum(p_new, axis=-1, keepdims=True)
        acc = (_dot_nt(p_hist.astype(BF16), vc_ref[0, h].astype(BF16))
               + _dot(p_new.astype(BF16), v_ref[0, :, hs].astype(BF16)))
        of_ref[0, :, hs] = (acc / l).astype(BF16)


def _mix_sample(u, q, k, v, lf, kc_t, vc_t, lfc_t, ph, w_pool, pool_scale, tc):
    bsz, t, _ = u.shape
    past = kc_t.shape[3]
    assert t <= LANES and past % tc == 0 and t >= POOL_HIST
    one = lambda b: (b, 0, 0)
    cache_spec = pl.BlockSpec((1, FOX_HEADS, FOX_HEAD_DIM, past), lambda b: (b, 0, 0, 0))
    kern = functools.partial(_mix_sample_kernel, past=past, t=t, tc=tc)
    return pl.pallas_call(
        kern,
        grid=(bsz,),
        in_specs=[
            pl.BlockSpec((1, t, D_POOL), one),
            pl.BlockSpec((1, t, D_FOX), one),
            pl.BlockSpec((1, t, D_FOX), one),
            pl.BlockSpec((1, t, D_FOX), one),
            pl.BlockSpec((1, t, FOX_HEADS), one),
            cache_spec,
            cache_spec,
            pl.BlockSpec((1, FOX_HEADS, past), one),
            pl.BlockSpec((1, POOL_HIST, D_POOL), one),
            _const_spec((len(POOL_WINDOWS), POOL_GROUP_W, POOL_GROUP_W)),
            _const_spec((1, D_POOL)),
        ],
        out_specs=[pl.BlockSpec((1, t, D_POOL), one), pl.BlockSpec((1, t, D_FOX), one)],
        out_shape=[jax.ShapeDtypeStruct((bsz, t, D_POOL), BF16),
                   jax.ShapeDtypeStruct((bsz, t, D_FOX), BF16)],
        scratch_shapes=[
            pltpu.VMEM((LANES, LANES), F32),
            pltpu.VMEM((FOX_HEADS, past), F32),
            pltpu.VMEM((POOL_BASE + t, D_POOL), F32),
            pltpu.VMEM((POOL_BASE + t, POOL_GROUP_W), F32),
            pltpu.VMEM((POOL_BASE + t, POOL_GROUP_W), F32),
        ],
        compiler_params=_params(("arbitrary",)),
        name="mix_sample",
    )(u, q, k, v, lf, kc_t, vc_t, lfc_t, ph, w_pool, pool_scale)


def _row_groups(nb, tm, sub):
    if tm >= sub:
        return [(b, b + 1, r0, r0 + sub) for b in range(nb) for r0 in range(0, tm, sub)]
    per = sub // tm
    return [(b0, b0 + per, 0, tm) for b0 in range(0, nb, per)]


def _memffn_kernel(x_ref, op_ref, of_ref, mk_ref, mv_ref, ch_ref, wo_ref, gxq_ref, wmq_ref, wmo_ref,
                   gffn_ref, wup_ref, cw_ref, cb_ref, wdown_ref, gfin_ref, y_ref, cs_ref, aext, mo_sc,
                   *, nb, tm, sub, final_norm):
    @pl.when(pl.program_id(1) == 0)
    def _hist():
        for b in range(nb):
            aext[b, CONV_PAD - (CONV_W - 1):CONV_PAD, :] = ch_ref[b]

    groups = _row_groups(nb, tm, sub)

    def load(ref, g):
        b0, b1, r0, r1 = g
        return ref[b0:b1, r0:r1, :].reshape(sub, ref.shape[-1])

    def pieces(g):
        b0, b1, r0, r1 = g
        n = r1 - r0
        return [(b, slice((b - b0) * n, (b - b0 + 1) * n), slice(r0, r1)) for b in range(b0, b1)]

    xs = [load(x_ref, g) + _dot(jnp.concatenate([load(op_ref, g), load(of_ref, g)], axis=1), wo_ref[...])
          for g in groups]

    mqs = [(_dot(_rms(x, gxq_ref[...]).astype(BF16), wmq_ref[...]) * (MEM_HEAD_DIM ** -0.5)).astype(BF16)
           for x in xs]
    for b in range(nb):
        for h in range(MEM_HEADS):
            hs = slice(h * MEM_HEAD_DIM, (h + 1) * MEM_HEAD_DIM)
            mk_h = mk_ref[b, :, hs].astype(BF16)
            mv_h = mv_ref[b, :, hs].astype(BF16)
            for gi, g in enumerate(groups):
                for pb, pr, _ in pieces(g):
                    if pb != b:
                        continue
                    s = _dot_nt(mqs[gi][pr, hs], mk_h)
                    m = jnp.max(s, axis=-1, keepdims=True)
                    p = jnp.exp(s - m)
                    l = jnp.sum(p, axis=-1, keepdims=True)
                    o = _dot(p.astype(BF16), mv_h) / l
                    mo_sc[gi * sub + pr.start:gi * sub + pr.stop, hs] = o.astype(BF16)
    x2s = [x + _dot(mo_sc[gi * sub:(gi + 1) * sub, :], wmo_ref[...]) for gi, x in enumerate(xs)]

    hfs = [_rms(x2, gffn_ref[...]).astype(BF16) for x2 in x2s]
    downs = []
    for g, hf in zip(groups, hfs):
        a = _dot(hf, wup_ref[:, 0:D_FF])
        b_up = _dot(hf, wup_ref[:, D_FF:2 * D_FF])
        convs = []
        for pb, pr, sr in pieces(g):
            r0, r1 = CONV_PAD + sr.start, CONV_PAD + sr.stop
            aext[pb, r0:r1, :] = a[pr, :]
            convs.append(aext[pb, r0 - 2:r1 - 2, :] * cw_ref[0:1, :]
                         + aext[pb, r0 - 1:r1 - 1, :] * cw_ref[1:2, :]
                         + a[pr, :] * cw_ref[2:3, :])
        conv = convs[0] if len(convs) == 1 else jnp.concatenate(convs, axis=0)
        conv = cb_ref[...] + conv
        gate = conv * (1.0 / (1.0 + jnp.exp(-conv)))
        downs.append(_dot((gate * b_up).astype(BF16), wdown_ref[...]))
    for b in range(nb):
        cs_ref[b] = aext[b, CONV_PAD + tm - (CONV_W - 1):CONV_PAD + tm, :]
        aext[b, 0:CONV_PAD, :] = aext[b, tm:tm + CONV_PAD, :]
    for g, x2, down in zip(groups, x2s, downs):
        b0, b1, r0, r1 = g
        x3 = x2 + down
        if final_norm:
            x3 = _rms(x3, gfin_ref[...])
        y_ref[b0:b1, r0:r1, :] = x3.reshape(b1 - b0, r1 - r0, D_MODEL)


def _memffn(x, o_pool, o_fox, mk, mv, conv_hist, w_o, g_xq, w_mq, w_mo, g_ffn, w_up, conv_w, conv_b,
            w_down, g_final, nb, tm, final_norm):
    bsz, t, _ = x.shape
    sub = min(nb * tm, 256)
    assert bsz % nb == 0 and t % tm == 0 and (nb * tm) % sub == 0 and (tm % sub == 0 or sub % tm == 0)
    blk = lambda b, i: (b, i, 0)
    full = lambda b, i: (b, 0, 0)
    kern = functools.partial(_memffn_kernel, nb=nb, tm=tm, sub=sub, final_norm=final_norm)
    return pl.pallas_call(
        kern,
        grid=(bsz // nb, t // tm),
        in_specs=[
            pl.BlockSpec((nb, tm, D_MODEL), blk),
            pl.BlockSpec((nb, tm, D_POOL), blk),
            pl.BlockSpec((nb, tm, D_FOX), blk),
            pl.BlockSpec((nb, N_MEM, D_MODEL), full),
            pl.BlockSpec((nb, N_MEM, D_MODEL), full),
            pl.BlockSpec((nb, CONV_W - 1, D_FF), full),
            _const_spec((D_MODEL, D_MODEL)),
            _const_spec((1, D_MODEL)),
            _const_spec((D_MODEL, D_MODEL)),
            _const_spec((D_MODEL, D_MODEL)),
            _const_spec((1, D_MODEL)),
            _const_spec((D_MODEL, 2 * D_FF)),
            _const_spec((CONV_W, D_FF)),
            _const_spec((1, D_FF)),
            _const_spec((D_FF, D_MODEL)),
            _const_spec((1, D_MODEL)),
        ],
        out_specs=[
            pl.BlockSpec((nb, tm, D_MODEL), blk),
            pl.BlockSpec((nb, CONV_W - 1, D_FF), full),
        ],
        out_shape=[
            jax.ShapeDtypeStruct((bsz, t, D_MODEL), F32),
            jax.ShapeDtypeStruct((bsz, CONV_W - 1, D_FF), F32),
        ],
        scratch_shapes=[
            pltpu.VMEM((nb, CONV_PAD + tm, D_FF), F32),
            pltpu.VMEM((nb * tm, D_MODEL), BF16),
        ],
        compiler_params=_params(("arbitrary", "arbitrary")),
        name="memffn",
    )(x, o_pool, o_fox, mk, mv, conv_hist, w_o, g_xq, w_mq, w_mo, g_ffn, w_up, conv_w, conv_b, w_down,
      g_final)


def _row_tile(n, target):
    tm = min(n, target)
    assert n % tm == 0
    return tm


def kernel(x_prompt, x_sample, cache_fox_k, cache_fox_v, cache_fox_logf, state_pool, state_ffn_conv, cache_mem_k, cache_mem_v, mem_prompt, g_mix, w_in, b_f, w_pool, pool_scale, w_o, g_xq, g_mkv, w_mq, w_mk, w_mv, w_mo, g_ffn, w_up, conv_w, conv_b, w_down, g_final):
    depth = w_in.shape[0]
    bp, tp, _ = x_prompt.shape
    bs, ts, _ = x_sample.shape
    past = cache_fox_k.shape[2]
    n_main = D_POOL + 3 * D_FOX

    xp, xs = x_prompt, x_sample
    outs = {name: [] for name in ("pk", "pv", "plf", "ppool", "pconv", "pmk", "pmv",
                                   "sk", "sv", "slf", "spool", "sconv")}
    for l in range(depth):
        last = l == depth - 1
        row2 = lambda a: a.reshape(1, -1).astype(F32)
        w_main = w_in[l][:, :n_main].astype(BF16)
        w_f = jnp.pad(w_in[l][:, n_main:], ((0, 0), (0, LANES - FOX_HEADS))).astype(BF16)
        bf = jnp.pad(b_f[l].astype(F32), (0, LANES - FOX_HEADS)).reshape(1, LANES)
        wp = w_pool[l].astype(BF16)
        ps = row2(pool_scale[l])
        ffn_w = (w_o[l].astype(BF16), row2(g_xq[l]), w_mq[l].astype(BF16), w_mo[l].astype(BF16),
                 row2(g_ffn[l]), w_up[l].astype(BF16), conv_w[l].astype(F32), row2(conv_b[l]),
                 w_down[l].astype(BF16), row2(g_final))

        mk, mv, mk4, mv4 = _memkv(mem_prompt.reshape(bp * N_MEM, D_MODEL), row2(g_mkv[l]),
                                  w_mk[l].astype(BF16), w_mv[l].astype(BF16), _row_tile(bp * N_MEM, 512))
        mk = mk.reshape(bp, N_MEM, D_MODEL)
        mv = mv.reshape(bp, N_MEM, D_MODEL)
        o_pool, u_tail, q, k_t, v_t, lf_t = _inproj(
            xp.reshape(bp * tp, D_MODEL), row2(g_mix[l]), w_main, w_f, bf, wp, ps, _row_tile(tp, 1024), tp,
            pool=True)
        o_fox = _fox_prompt(q.reshape(bp, tp, D_FOX), k_t, v_t, lf_t, _row_tile(tp, 256))
        xp, pconv = _memffn(xp, o_pool.reshape(bp, tp, D_POOL), o_fox, mk, mv,
                            jnp.zeros((bp, CONV_W - 1, D_FF), F32), *ffn_w,
                            nb=1, tm=_row_tile(tp, 512), final_norm=last)
        outs["pk"].append(jnp.transpose(k_t, (0, 3, 1, 2)))
        outs["pv"].append(jnp.transpose(v_t, (0, 3, 1, 2)))
        outs["plf"].append(jnp.transpose(lf_t, (0, 2, 1)))
        outs["ppool"].append(u_tail[:, POOL_CARRY - POOL_HIST:, :])
        outs["pconv"].append(pconv)
        outs["pmk"].append(mk4.reshape(bp, N_MEM, MEM_HEADS, MEM_HEAD_DIM))
        outs["pmv"].append(mv4.reshape(bp, N_MEM, MEM_HEADS, MEM_HEAD_DIM))

        u, q, k, v, lf = _inproj(xs.reshape(bs * ts, D_MODEL), row2(g_mix[l]), w_main, w_f, bf,
                                 wp, ps, _row_tile(bs * ts, 512), ts, pool=False)
        u = u.reshape(bs, ts, D_POOL)
        k = k.reshape(bs, ts, D_FOX)
        v = v.reshape(bs, ts, D_FOX)
        lf = lf.reshape(bs, ts, FOX_HEADS)
        o_pool, o_fox = _mix_sample(u, q.reshape(bs, ts, D_FOX), k, v, lf,
                                    jnp.transpose(cache_fox_k[l], (0, 2, 3, 1)),
                                    jnp.transpose(cache_fox_v[l], (0, 2, 3, 1)),
                                    jnp.transpose(cache_fox_logf[l].astype(F32), (0, 2, 1)),
                                    state_pool[l], wp, ps, tc=512)
        xs, sconv = _memffn(xs, o_pool, o_fox, cache_mem_k[l].reshape(bs, N_MEM, D_MODEL),
                            cache_mem_v[l].reshape(bs, N_MEM, D_MODEL), state_ffn_conv[l], *ffn_w,
                            nb=_row_tile(bs, 4), tm=ts, final_norm=last)
        outs["sk"].append(k.reshape(bs, ts, FOX_HEADS, FOX_HEAD_DIM))
        outs["sv"].append(v.reshape(bs, ts, FOX_HEADS, FOX_HEAD_DIM))
        outs["slf"].append(lf)
        outs["spool"].append(u[:, ts - POOL_HIST:, :])
        outs["sconv"].append(sconv)

    st = {name: jnp.stack(vals) for name, vals in outs.items()}
    return (xp, xs, st["pk"], st["pv"], st["plf"], st["ppool"], st["pconv"], st["pmk"], st["pmv"],
            st["sk"], st["sv"], st["slf"], st["spool"], st["sconv"])
```

```python
import functools
import math

import jax
import jax.numpy as jnp
from jax import lax
from jax.experimental import pallas as pl
from jax.experimental.pallas import tpu as pltpu

D_MODEL = 1024
D_POOL = 512
POOL_WINDOWS = (2, 4, 8, 16)
POOL_GROUP_W = 128
POOL_HIST = 15
POOL_CARRY = 16
POOL_PAD = 8
POOL_BASE = POOL_PAD + POOL_CARRY
D_FOX = 512
FOX_HEADS = 8
FOX_HEAD_DIM = 64
FOX_PAIRS = FOX_HEADS // 2
N_MEM = 256
MEM_HEADS = 4
MEM_HEAD_DIM = 256
D_FF = 2816
CONV_W = 3
CONV_PAD = 8
EPS = 1e-6
LOG2E = math.log2(math.e)
LANES = 128
VMEM_LIMIT = 60 * 1024 * 1024

F32 = jnp.float32
BF16 = jnp.bfloat16


def _dot(a, b):
    return jnp.dot(a, b, preferred_element_type=F32)


def _dot_nt(a, b):
    return lax.dot_general(a, b, (((1,), (1,)), ((), ())), preferred_element_type=F32)


def _rms(x, g):
    ms = jnp.mean(x * x, axis=-1, keepdims=True)
    return (x * lax.rsqrt(ms + EPS)) * g


def _const_spec(shape):
    nd = len(shape)
    return pl.BlockSpec(shape, lambda *_: (0,) * nd, pipeline_mode=pl.Buffered(1))


def _params(sem):
    return pltpu.CompilerParams(dimension_semantics=sem, vmem_limit_bytes=VMEM_LIMIT)


def _tril(n):
    r = lax.broadcasted_iota(jnp.int32, (n, n), 0)
    c = lax.broadcasted_iota(jnp.int32, (n, n), 1)
    return jnp.where(r >= c, 1.0, 0.0).astype(BF16)


def _triu(n):
    r = lax.broadcasted_iota(jnp.int32, (n, n), 0)
    c = lax.broadcasted_iota(jnp.int32, (n, n), 1)
    return jnp.where(r <= c, 1.0, 0.0).astype(BF16)


def _split3(x):
    hi = x.astype(BF16)
    r1 = x - hi.astype(F32)
    mid = r1.astype(BF16)
    lo = (r1 - mid.astype(F32)).astype(BF16)
    return hi, mid, lo


def _cumsum_rows(x, tril, carry):
    hi, mid, lo = _split3(x)
    return _dot(tril, hi) + _dot(tril, mid) + _dot(tril, lo) + carry


def _cumsum_lanes(x, triu, carry):
    hi, mid, lo = _split3(x)
    return _dot(hi, triu) + _dot(mid, triu) + _dot(lo, triu) + carry


def _pool_mixer(uext_ref, tmp_a, tmp_b, tq, t0, pos0, wpool_ref, pscale_ref, out_ref):
    lo, hi = POOL_PAD, POOL_BASE + tq
    t_idx = t0 + lax.broadcasted_iota(jnp.int32, (tq, 1), 0)
    for g, w in enumerate(POOL_WINDOWS):
        sl = slice(g * POOL_GROUP_W, (g + 1) * POOL_GROUP_W)
        src, bufs, d = uext_ref, (tmp_a, tmp_b), 1
        col = sl
        while 2 * d < w:
            dst = bufs[0]
            dst[lo:hi, :] = src[lo:hi, col] + src[lo - d:hi - d, col]
            src, bufs, d, col = dst, (bufs[1], bufs[0]), 2 * d, slice(0, POOL_GROUP_W)
        s = src[POOL_BASE:hi, col] + src[POOL_BASE - d:hi - d, col]
        u = uext_ref[POOL_BASE:hi, sl]
        count = jnp.minimum(pos0 + t_idx + 1, w).astype(F32)
        diff = (s / count - u).astype(BF16)
        y = _dot(diff, wpool_ref[g]) * pscale_ref[:, sl]
        out_ref[:, sl] = y.astype(out_ref.dtype)


def _fold_lanes(x, op):
    parts = [x[:, c:c + LANES] for c in range(0, x.shape[1], LANES)]
    return functools.reduce(op, parts)


def _inproj_kernel(x_ref, g_ref, w_ref, wf_ref, bf_ref, wpool_ref, pscale_ref, *refs, tm, sub, bps, pool):
    if pool:
        op_ref, ut_ref, q_ref, k_ref, v_ref, lf_ref, uext, tmp_a, tmp_b = refs
    else:
        u_ref, q_ref, k_ref, v_ref, lf_ref = refs
    if pool:
        blk = pl.program_id(0) % bps

        @pl.when(blk == 0)
        def _no_history():
            uext[0:POOL_BASE, :] = jnp.zeros((POOL_BASE, D_POOL), F32)
            tmp_a[0:POOL_PAD, :] = jnp.zeros((POOL_PAD, POOL_GROUP_W), F32)
            tmp_b[0:POOL_PAD, :] = jnp.zeros((POOL_PAD, POOL_GROUP_W), F32)

    subs = [slice(r0, r0 + sub) for r0 in range(0, tm, sub)]
    hs = [_rms(x_ref[rs, :], g_ref[...]).astype(BF16) for rs in subs]
    for rs, h in zip(subs, hs):
        u = _dot(h, w_ref[:, 0:D_POOL])
        if pool:
            uext[POOL_BASE + rs.start:POOL_BASE + rs.stop, :] = u
        else:
            u_ref[rs, :] = u
    if pool:
        _pool_mixer(uext, tmp_a, tmp_b, tm, blk * tm, 0, wpool_ref, pscale_ref, op_ref)
        tail = uext[tm + POOL_BASE - POOL_CARRY:tm + POOL_BASE, :]
        ut_ref[0] = tail
        uext[POOL_BASE - POOL_CARRY:POOL_BASE, :] = tail
    for rs, h in zip(subs, hs):
        qkv = _dot(h, w_ref[:, D_POOL:D_POOL + 3 * D_FOX])
        q_ref[rs, :] = (qkv[:, 0:D_FOX] * (FOX_HEAD_DIM ** -0.5 * LOG2E)).astype(BF16)
        k = qkv[:, D_FOX:2 * D_FOX]
        v = qkv[:, 2 * D_FOX:3 * D_FOX]
        zf = _dot(h, wf_ref[...]) + bf_ref[...]
        lf = -(jnp.maximum(-zf, 0.0) + jnp.log1p(jnp.exp(-jnp.abs(zf))))
        if pool:
            k_t = k.T.reshape(FOX_HEADS, FOX_HEAD_DIM, sub)
            v_t = v.T.reshape(FOX_HEADS, FOX_HEAD_DIM, sub)
            k_ref[0, :, :, rs] = k_t
            v_ref[0, :, :, rs] = v_t
            lf_ref[0, :, rs] = lf.T[0:FOX_HEADS, :]
        else:
            k_ref[rs, :] = k
            v_ref[rs, :] = v
            lf_ref[rs, :] = lf[:, 0:FOX_HEADS]


def _inproj(x2d, g, w_main, w_f, b_f, w_pool, pool_scale, tm, seq_len, pool):
    n = x2d.shape[0]
    bps = seq_len // tm
    row = lambda i: (i, 0)
    if pool:
        nb = n // seq_len
        kv_spec = pl.BlockSpec((1, FOX_HEADS, FOX_HEAD_DIM, tm), lambda i: (i // bps, 0, 0, i % bps))
        kv_shape = jax.ShapeDtypeStruct((nb, FOX_HEADS, FOX_HEAD_DIM, seq_len), F32)
        out_specs = [pl.BlockSpec((tm, D_POOL), row),
                     pl.BlockSpec((1, POOL_CARRY, D_POOL), lambda i: (i // bps, 0, 0)),
                     pl.BlockSpec((tm, D_FOX), row), kv_spec, kv_spec,
                     pl.BlockSpec((1, FOX_HEADS, tm), lambda i: (i // bps, 0, i % bps))]
        out_shape = [jax.ShapeDtypeStruct((n, D_POOL), BF16),
                     jax.ShapeDtypeStruct((nb, POOL_CARRY, D_POOL), F32),
                     jax.ShapeDtypeStruct((n, D_FOX), BF16), kv_shape, kv_shape,
                     jax.ShapeDtypeStruct((nb, FOX_HEADS, seq_len), F32)]
        scratch = [pltpu.VMEM((POOL_BASE + tm, D_POOL), F32),
                   pltpu.VMEM((POOL_BASE + tm, POOL_GROUP_W), F32),
                   pltpu.VMEM((POOL_BASE + tm, POOL_GROUP_W), F32)]
    else:
        out_specs = [pl.BlockSpec((tm, D_POOL), row), pl.BlockSpec((tm, D_FOX), row),
                     pl.BlockSpec((tm, D_FOX), row), pl.BlockSpec((tm, D_FOX), row),
                     pl.BlockSpec((tm, FOX_HEADS), row)]
        out_shape = [jax.ShapeDtypeStruct((n, D_POOL), F32), jax.ShapeDtypeStruct((n, D_FOX), BF16),
                     jax.ShapeDtypeStruct((n, D_FOX), F32), jax.ShapeDtypeStruct((n, D_FOX), F32),
                     jax.ShapeDtypeStruct((n, FOX_HEADS), F32)]
        scratch = []
    return pl.pallas_call(
        functools.partial(_inproj_kernel, tm=tm, sub=min(tm, 512), bps=bps, pool=pool),
        grid=(n // tm,),
        in_specs=[
            pl.BlockSpec((tm, D_MODEL), row),
            _const_spec((1, D_MODEL)),
            _const_spec((D_MODEL, D_POOL + 3 * D_FOX)),
            _const_spec((D_MODEL, LANES)),
            _const_spec((1, LANES)),
            _const_spec((len(POOL_WINDOWS), POOL_GROUP_W, POOL_GROUP_W)),
            _const_spec((1, D_POOL)),
        ],
        out_specs=out_specs,
        out_shape=out_shape,
        scratch_shapes=scratch,
        compiler_params=_params(("arbitrary",)),
        name="inproj_pool" if pool else "inproj",
    )(x2d, g, w_main, w_f, b_f, w_pool, pool_scale)


def _memkv_kernel(m_ref, g_ref, wk_ref, wv_ref, mk_ref, mv_ref, mk4_ref, mv4_ref):
    h = _rms(m_ref[...], g_ref[...]).astype(BF16)
    mk = _dot(h, wk_ref[...])
    mv = _dot(h, wv_ref[...])
    mk_ref[...] = mk
    mv_ref[...] = mv
    for hd in range(MEM_HEADS):
        hs = slice(hd * MEM_HEAD_DIM, (hd + 1) * MEM_HEAD_DIM)
        mk4_ref[:, hd, :] = mk[:, hs]
        mv4_ref[:, hd, :] = mv[:, hs]


def _memkv(m2d, g, wk, wv, tm):
    n = m2d.shape[0]
    row = lambda i: (i, 0)
    row3 = lambda i: (i, 0, 0)
    return pl.pallas_call(
        _memkv_kernel,
        grid=(n // tm,),
        in_specs=[
            pl.BlockSpec((tm, D_MODEL), row),
            _const_spec((1, D_MODEL)),
            _const_spec((D_MODEL, D_MODEL)),
            _const_spec((D_MODEL, D_MODEL)),
        ],
        out_specs=[pl.BlockSpec((tm, D_MODEL), row), pl.BlockSpec((tm, D_MODEL), row),
                   pl.BlockSpec((tm, MEM_HEADS, MEM_HEAD_DIM), row3),
                   pl.BlockSpec((tm, MEM_HEADS, MEM_HEAD_DIM), row3)],
        out_shape=[jax.ShapeDtypeStruct((n, D_MODEL), F32)] * 2
        + [jax.ShapeDtypeStruct((n, MEM_HEADS, MEM_HEAD_DIM), F32)] * 2,
        compiler_params=_params(("arbitrary",)),
        name="memkv",
    )(m2d, g, wk, wv)


def _fox_prompt_kernel(q_ref, k_ref, v_ref, lf_ref, o_ref, ccol, crow, cpad, *, nq, tq):
    pair = pl.program_id(1)
    lane = lax.broadcasted_iota(jnp.int32, (1, LANES), 1)
    low = lane < FOX_HEAD_DIM

    @pl.when(pair == 0)
    def _cumsum():
        triu = _triu(tq)
        cpad[...] = jnp.zeros_like(cpad)
        carry = jnp.zeros((FOX_HEADS, 1), F32)
        for b in range(nq):
            c = _cumsum_lanes(lf_ref[0, :, b * tq:(b + 1) * tq], triu, carry)
            carry = c[:, tq - 1:tq]
            c2 = c * LOG2E
            crow[b] = c2
            cpad[0:FOX_HEADS, :] = c2
            ccol[b] = cpad[...].T

    r_idx = lax.broadcasted_iota(jnp.int32, (tq, tq), 0)
    c_idx = lax.broadcasted_iota(jnp.int32, (tq, tq), 1)
    causal = r_idx >= c_idx
    zero = jnp.zeros((), BF16)
    ones_row = jnp.where(lax.broadcasted_iota(jnp.int32, (FOX_HEAD_DIM, 1), 0) == 0, 1.0, 0.0).astype(BF16)
    for i in reversed(range(nq)):
        keys = (i + 1) * tq
        qp = q_ref[0, i * tq:(i + 1) * tq, :]
        cblk = ccol[i]
        o_blk = None
        for odd in range(2):
            head = 2 * pair + odd
            q_h = jnp.where(low, zero, qp) if odd else jnp.where(low, qp, zero)
            v_own = v_ref[0, odd, :, 0:keys].astype(BF16)
            v_pad = jnp.broadcast_to(ones_row, v_own.shape)
            v_h = jnp.concatenate([v_pad, v_own] if odd else [v_own, v_pad], axis=0)
            cq = jnp.sum(jnp.where(lane == head, cblk, 0.0), axis=-1, keepdims=True)
            s_list = []
            for j in range(i + 1):
                k_j = k_ref[0, :, :, j * tq:(j + 1) * tq].reshape(LANES, tq).astype(BF16)
                s = _dot(q_h, k_j) - crow[j, pl.ds(head, 1), :]
                if j == i:
                    s = jnp.where(causal, s, -jnp.inf)
                s_list.append(s)
            m = functools.reduce(jnp.maximum, [_fold_lanes(s, jnp.maximum) for s in s_list])
            m_row = jnp.max(m, axis=-1, keepdims=True) + cq
            r = cq - m_row
            p_cat = jnp.concatenate([jnp.exp2(s + r).astype(BF16) for s in s_list], axis=1)
            acc = _dot_nt(p_cat, v_h)
            den = 0 if odd else FOX_HEAD_DIM
            o_h = acc / acc[:, den:den + 1]
            o_blk = o_h if o_blk is None else jnp.where(low, o_blk, o_h)
        o_ref[0, i * tq:(i + 1) * tq, :] = o_blk.astype(BF16)


def _fox_prompt(q, k_t, v_t, lf_t, tq):
    bsz, t, _ = q.shape
    nq = t // tq
    col = lambda b, p: (b, 0, p)
    kv_spec = pl.BlockSpec((1, 2, FOX_HEAD_DIM, t), lambda b, p: (b, p, 0, 0))
    return pl.pallas_call(
        functools.partial(_fox_prompt_kernel, nq=nq, tq=tq),
        grid=(bsz, FOX_PAIRS),
        in_specs=[
            pl.BlockSpec((1, t, LANES), col),
            kv_spec,
            kv_spec,
            pl.BlockSpec((1, FOX_HEADS, t), lambda b, p: (b, 0, 0)),
        ],
        out_specs=pl.BlockSpec((1, t, LANES), col),
        out_shape=jax.ShapeDtypeStruct((bsz, t, D_FOX), BF16),
        scratch_shapes=[
            pltpu.VMEM((nq, tq, LANES), F32),
            pltpu.VMEM((nq, FOX_HEADS, tq), F32),
            pltpu.VMEM((LANES, tq), F32),
        ],
        compiler_params=_params(("arbitrary", "arbitrary")),
        name="fox_prompt",
    )(q, k_t, v_t, lf_t)


def _mix_sample_kernel(u_ref, q_ref, k_ref, v_ref, lf_ref, kc_ref, vc_ref, lfc_ref, ph_ref,
                       wpool_ref, pscale_ref, op_ref, of_ref, cbuf, crow, uext, tmp_a, tmp_b,
                       *, past, t, tc):
    nblk = past // tc
    triu = _triu(tc)
    carry = jnp.zeros((FOX_HEADS, 1), F32)
    for b in range(nblk):
        c = _cumsum_lanes(lfc_ref[0, :, b * tc:(b + 1) * tc], triu, carry)
        carry = c[:, tc - 1:tc]
        crow[:, b * tc:(b + 1) * tc] = c * LOG2E
    cbuf[...] = jnp.zeros_like(cbuf)
    cbuf[0:t, 0:FOX_HEADS] = lf_ref[0]
    cnew = _cumsum_rows(cbuf[...], _tril(LANES), jnp.zeros((1, LANES), F32))
    cnew_row = (cnew.T[0:FOX_HEADS, 0:t] + carry) * LOG2E

    uext[0:POOL_BASE - POOL_HIST, :] = jnp.zeros((POOL_BASE - POOL_HIST, D_POOL), F32)
    tmp_a[0:POOL_PAD, :] = jnp.zeros((POOL_PAD, POOL_GROUP_W), F32)
    tmp_b[0:POOL_PAD, :] = jnp.zeros((POOL_PAD, POOL_GROUP_W), F32)
    uext[POOL_BASE - POOL_HIST:POOL_BASE, :] = ph_ref[0]
    uext[POOL_BASE:POOL_BASE + t, :] = u_ref[0]
    _pool_mixer(uext, tmp_a, tmp_b, t, 0, past, wpool_ref, pscale_ref, op_ref.at[0])

    r_idx = lax.broadcasted_iota(jnp.int32, (t, t), 0)
    c_idx = lax.broadcasted_iota(jnp.int32, (t, t), 1)
    causal = r_idx >= c_idx
    for h in range(FOX_HEADS):
        hs = slice(h * FOX_HEAD_DIM, (h + 1) * FOX_HEAD_DIM)
        q_h = q_ref[0, :, hs]
        cq = (cnew[0:t, h:h + 1] + carry[h:h + 1, :]) * LOG2E
        s_hist = _dot(q_h, kc_ref[0, h].astype(BF16)) - crow[h:h + 1, :]
        s_new = _dot_nt(q_h, k_ref[0, :, hs].astype(BF16)) - cnew_row[h:h + 1, :]
        s_new = jnp.where(causal, s_new, -jnp.inf)
        m_row = jnp.maximum(jnp.max(s_hist, axis=-1, keepdims=True),
                            jnp.max(s_new, axis=-1, keepdims=True)) + cq
        r = cq - m_row
        p_hist = jnp.exp2(s_hist + r)
        p_new = jnp.exp2(s_new + r)
        l = jnp.sum(p_hist, axis=-1, keepdims=True) + jnp.sum(p_new, axis=-1, keepdims=True)
        acc = (_dot_nt(p_hist.astype(BF16), vc_ref[0, h].astype(BF16))
               + _dot(p_new.astype(BF16), v_ref[0, :, hs].astype(BF16)))
        of_ref[0, :, hs] = (acc / l).astype(BF16)


def _mix_sample(u, q, k, v, lf, kc_t, vc_t, lfc_t, ph, w_pool, pool_scale, tc):
    bsz, t, _ = u.shape
    past = kc_t.shape[3]
    assert t <= LANES and past % tc == 0 and t >= POOL_HIST
    one = lambda b: (b, 0, 0)
    cache_spec = pl.BlockSpec((1, FOX_HEADS, FOX_HEAD_DIM, past), lambda b: (b, 0, 0, 0))
    kern = functools.partial(_mix_sample_kernel, past=past, t=t, tc=tc)
    return pl.pallas_call(
        kern,
        grid=(bsz,),
        in_specs=[
            pl.BlockSpec((1, t, D_POOL), one),
            pl.BlockSpec((1, t, D_FOX), one),
            pl.BlockSpec((1, t, D_FOX), one),
            pl.BlockSpec((1, t, D_FOX), one),
            pl.BlockSpec((1, t, FOX_HEADS), one),
            cache_spec,
            cache_spec,
            pl.BlockSpec((1, FOX_HEADS, past), one),
            pl.BlockSpec((1, POOL_HIST, D_POOL), one),
            _const_spec((len(POOL_WINDOWS), POOL_GROUP_W, POOL_GROUP_W)),
            _const_spec((1, D_POOL)),
        ],
        out_specs=[pl.BlockSpec((1, t, D_POOL), one), pl.BlockSpec((1, t, D_FOX), one)],
        out_shape=[jax.ShapeDtypeStruct((bsz, t, D_POOL), BF16),
                   jax.ShapeDtypeStruct((bsz, t, D_FOX), BF16)],
        scratch_shapes=[
            pltpu.VMEM((LANES, LANES), F32),
            pltpu.VMEM((FOX_HEADS, past), F32),
            pltpu.VMEM((POOL_BASE + t, D_POOL), F32),
            pltpu.VMEM((POOL_BASE + t, POOL_GROUP_W), F32),
            pltpu.VMEM((POOL_BASE + t, POOL_GROUP_W), F32),
        ],
        compiler_params=_params(("arbitrary",)),
        name="mix_sample",
    )(u, q, k, v, lf, kc_t, vc_t, lfc_t, ph, w_pool, pool_scale)


def _row_groups(nb, tm, sub):
    if tm >= sub:
        return [(b, b + 1, r0, r0 + sub) for b in range(nb) for r0 in range(0, tm, sub)]
    per = sub // tm
    return [(b0, b0 + per, 0, tm) for b0 in range(0, nb, per)]


def _memffn_kernel(x_ref, op_ref, of_ref, mk_ref, mv_ref, ch_ref, wo_ref, gxq_ref, wmq_ref, wmo_ref,
                   gffn_ref, wup_ref, cw_ref, cb_ref, wdown_ref, gfin_ref, y_ref, cs_ref, aext, mo_sc,
                   *, nb, tm, sub, final_norm):
    @pl.when(pl.program_id(1) == 0)
    def _hist():
        for b in range(nb):
            aext[b, CONV_PAD - (CONV_W - 1):CONV_PAD, :] = ch_ref[b]

    groups = _row_groups(nb, tm, sub)

    def load(ref, g):
        b0, b1, r0, r1 = g
        return ref[b0:b1, r0:r1, :].reshape(sub, ref.shape[-1])

    def pieces(g):
        b0, b1, r0, r1 = g
        n = r1 - r0
        return [(b, slice((b - b0) * n, (b - b0 + 1) * n), slice(r0, r1)) for b in range(b0, b1)]

    xs = [load(x_ref, g) + _dot(jnp.concatenate([load(op_ref, g), load(of_ref, g)], axis=1), wo_ref[...])
          for g in groups]

    mqs = [(_dot(_rms(x, gxq_ref[...]).astype(BF16), wmq_ref[...]) * (MEM_HEAD_DIM ** -0.5)).astype(BF16)
           for x in xs]
    for b in range(nb):
        for h in range(MEM_HEADS):
            hs = slice(h * MEM_HEAD_DIM, (h + 1) * MEM_HEAD_DIM)
            mk_h = mk_ref[b, :, hs].astype(BF16)
            mv_h = mv_ref[b, :, hs].astype(BF16)
            for gi, g in enumerate(groups):
                for pb, pr, _ in pieces(g):
                    if pb != b:
                        continue
                    s = _dot_nt(mqs[gi][pr, hs], mk_h)
                    m = jnp.max(s, axis=-1, keepdims=True)
                    p = jnp.exp(s - m)
                    l = jnp.sum(p, axis=-1, keepdims=True)
                    o = _dot(p.astype(BF16), mv_h) / l
                    mo_sc[gi * sub + pr.start:gi * sub + pr.stop, hs] = o.astype(BF16)
    x2s = [x + _dot(mo_sc[gi * sub:(gi + 1) * sub, :], wmo_ref[...]) for gi, x in enumerate(xs)]

    hfs = [_rms(x2, gffn_ref[...]).astype(BF16) for x2 in x2s]
    downs = []
    for g, hf in zip(groups, hfs):
        a = _dot(hf, wup_ref[:, 0:D_FF])
        b_up = _dot(hf, wup_ref[:, D_FF:2 * D_FF])
        convs = []
        for pb, pr, sr in pieces(g):
            r0, r1 = CONV_PAD + sr.start, CONV_PAD + sr.stop
            aext[pb, r0:r1, :] = a[pr, :]
            convs.append(aext[pb, r0 - 2:r1 - 2, :] * cw_ref[0:1, :]
                         + aext[pb, r0 - 1:r1 - 1, :] * cw_ref[1:2, :]
                         + a[pr, :] * cw_ref[2:3, :])
        conv = convs[0] if len(convs) == 1 else jnp.concatenate(convs, axis=0)
        conv = cb_ref[...] + conv
        gate = conv * (1.0 / (1.0 + jnp.exp(-conv)))
        downs.append(_dot((gate * b_up).astype(BF16), wdown_ref[...]))
    for b in range(nb):
        cs_ref[b] = aext[b, CONV_PAD + tm - (CONV_W - 1):CONV_PAD + tm, :]
        aext[b, 0:CONV_PAD, :] = aext[b, tm:tm + CONV_PAD, :]
    for g, x2, down in zip(groups, x2s, downs):
        b0, b1, r0, r1 = g
        x3 = x2 + down
        if final_norm:
            x3 = _rms(x3, gfin_ref[...])
        y_ref[b0:b1, r0:r1, :] = x3.reshape(b1 - b0, r1 - r0, D_MODEL)


def _memffn(x, o_pool, o_fox, mk, mv, conv_hist, w_o, g_xq, w_mq, w_mo, g_ffn, w_up, conv_w, conv_b,
            w_down, g_final, nb, tm, final_norm):
    bsz, t, _ = x.shape
    sub = min(nb * tm, 256)
    assert bsz % nb == 0 and t % tm == 0 and (nb * tm) % sub == 0 and (tm % sub == 0 or sub % tm == 0)
    blk = lambda b, i: (b, i, 0)
    full = lambda b, i: (b, 0, 0)
    kern = functools.partial(_memffn_kernel, nb=nb, tm=tm, sub=sub, final_norm=final_norm)
    return pl.pallas_call(
        kern,
        grid=(bsz // nb, t // tm),
        in_specs=[
            pl.BlockSpec((nb, tm, D_MODEL), blk),
            pl.BlockSpec((nb, tm, D_POOL), blk),
            pl.BlockSpec((nb, tm, D_FOX), blk),
            pl.BlockSpec((nb, N_MEM, D_MODEL), full),
            pl.BlockSpec((nb, N_MEM, D_MODEL), full),
            pl.BlockSpec((nb, CONV_W - 1, D_FF), full),
            _const_spec((D_MODEL, D_MODEL)),
            _const_spec((1, D_MODEL)),
            _const_spec((D_MODEL, D_MODEL)),
            _const_spec((D_MODEL, D_MODEL)),
            _const_spec((1, D_MODEL)),
            _const_spec((D_MODEL, 2 * D_FF)),
            _const_spec((CONV_W, D_FF)),
            _const_spec((1, D_FF)),
            _const_spec((D_FF, D_MODEL)),
            _const_spec((1, D_MODEL)),
        ],
        out_specs=[
            pl.BlockSpec((nb, tm, D_MODEL), blk),
            pl.BlockSpec((nb, CONV_W - 1, D_FF), full),
        ],
        out_shape=[
            jax.ShapeDtypeStruct((bsz, t, D_MODEL), F32),
            jax.ShapeDtypeStruct((bsz, CONV_W - 1, D_FF), F32),
        ],
        scratch_shapes=[
            pltpu.VMEM((nb, CONV_PAD + tm, D_FF), F32),
            pltpu.VMEM((nb * tm, D_MODEL), BF16),
        ],
        compiler_params=_params(("arbitrary", "arbitrary")),
        name="memffn",
    )(x, o_pool, o_fox, mk, mv, conv_hist, w_o, g_xq, w_mq, w_mo, g_ffn, w_up, conv_w, conv_b, w_down,
      g_final)


def _row_tile(n, target):
    tm = min(n, target)
    assert n % tm == 0
    return tm


def kernel(x_prompt, x_sample, cache_fox_k, cache_fox_v, cache_fox_logf, state_pool, state_ffn_conv, cache_mem_k, cache_mem_v, mem_prompt, g_mix, w_in, b_f, w_pool, pool_scale, w_o, g_xq, g_mkv, w_mq, w_mk, w_mv, w_mo, g_ffn, w_up, conv_w, conv_b, w_down, g_final):
    depth = w_in.shape[0]
    bp, tp, _ = x_prompt.shape
    bs, ts, _ = x_sample.shape
    past = cache_fox_k.shape[2]
    n_main = D_POOL + 3 * D_FOX

    xp, xs = x_prompt, x_sample
    outs = {name: [] for name in ("pk", "pv", "plf", "ppool", "pconv", "pmk", "pmv",
                                   "sk", "sv", "slf", "spool", "sconv")}
    for l in range(depth):
        last = l == depth - 1
        row2 = lambda a: a.reshape(1, -1).astype(F32)
        w_main = w_in[l][:, :n_main].astype(BF16)
        w_f = jnp.pad(w_in[l][:, n_main:], ((0, 0), (0, LANES - FOX_HEADS))).astype(BF16)
        bf = jnp.pad(b_f[l].astype(F32), (0, LANES - FOX_HEADS)).reshape(1, LANES)
        wp = w_pool[l].astype(BF16)
        ps = row2(pool_scale[l])
        ffn_w = (w_o[l].astype(BF16), row2(g_xq[l]), w_mq[l].astype(BF16), w_mo[l].astype(BF16),
                 row2(g_ffn[l]), w_up[l].astype(BF16), conv_w[l].astype(F32), row2(conv_b[l]),
                 w_down[l].astype(BF16), row2(g_final))

        mk, mv, mk4, mv4 = _memkv(mem_prompt.reshape(bp * N_MEM, D_MODEL), row2(g_mkv[l]),
                                  w_mk[l].astype(BF16), w_mv[l].astype(BF16), _row_tile(bp * N_MEM, 512))
        mk = mk.reshape(bp, N_MEM, D_MODEL)
        mv = mv.reshape(bp, N_MEM, D_MODEL)
        o_pool, u_tail, q, k_t, v_t, lf_t = _inproj(
            xp.reshape(bp * tp, D_MODEL), row2(g_mix[l]), w_main, w_f, bf, wp, ps, _row_tile(tp, 1024), tp,
            pool=True)
        o_fox = _fox_prompt(q.reshape(bp, tp, D_FOX), k_t, v_t, lf_t, _row_tile(tp, 256))
        xp, pconv = _memffn(xp, o_pool.reshape(bp, tp, D_POOL), o_fox, mk, mv,
                            jnp.zeros((bp, CONV_W - 1, D_FF), F32), *ffn_w,
                            nb=1, tm=_row_tile(tp, 512), final_norm=last)
        outs["pk"].append(jnp.transpose(k_t, (0, 3, 1, 2)))
        outs["pv"].append(jnp.transpose(v_t, (0, 3, 1, 2)))
        outs["plf"].append(jnp.transpose(lf_t, (0, 2, 1)))
        outs["ppool"].append(u_tail[:, POOL_CARRY - POOL_HIST:, :])
        outs["pconv"].append(pconv)
        outs["pmk"].append(mk4.reshape(bp, N_MEM, MEM_HEADS, MEM_HEAD_DIM))
        outs["pmv"].append(mv4.reshape(bp, N_MEM, MEM_HEADS, MEM_HEAD_DIM))

        u, q, k, v, lf = _inproj(xs.reshape(bs * ts, D_MODEL), row2(g_mix[l]), w_main, w_f, bf,
                                 wp, ps, _row_tile(bs * ts, 512), ts, pool=False)
        u = u.reshape(bs, ts, D_POOL)
        k = k.reshape(bs, ts, D_FOX)
        v = v.reshape(bs, ts, D_FOX)
        lf = lf.reshape(bs, ts, FOX_HEADS)
        o_pool, o_fox = _mix_sample(u, q.reshape(bs, ts, D_FOX), k, v, lf,
                                    jnp.transpose(cache_fox_k[l], (0, 2, 3, 1)),
                                    jnp.transpose(cache_fox_v[l], (0, 2, 3, 1)),
                                    jnp.transpose(cache_fox_logf[l].astype(F32), (0, 2, 1)),
                                    state_pool[l], wp, ps, tc=512)
        xs, sconv = _memffn(xs, o_pool, o_fox, cache_mem_k[l].reshape(bs, N_MEM, D_MODEL),
                            cache_mem_v[l].reshape(bs, N_MEM, D_MODEL), state_ffn_conv[l], *ffn_w,
                            nb=_row_tile(bs, 4), tm=ts, final_norm=last)
        outs["sk"].append(k.reshape(bs, ts, FOX_HEADS, FOX_HEAD_DIM))
        outs["sv"].append(v.reshape(bs, ts, FOX_HEADS, FOX_HEAD_DIM))
        outs["slf"].append(lf)
        outs["spool"].append(u[:, ts - POOL_HIST:, :])
        outs["sconv"].append(sconv)

    st = {name: jnp.stack(vals) for name, vals in outs.items()}
    return (xp, xs, st["pk"], st["pv"], st["plf"], st["ppool"], st["pconv"], st["pmk"], st["pmv"],
            st["sk"], st["sv"], st["slf"], st["spool"], st["sconv"])
```

```python
import functools
import math

import jax
import jax.numpy as jnp
from jax import lax
from jax.experimental import pallas as pl
from jax.experimental.pallas import tpu as pltpu

D_MODEL = 1024
D_POOL = 512
POOL_WINDOWS = (2, 4, 8, 16)
POOL_GROUP_W = 128
POOL_HIST = 15
POOL_CARRY = 16
POOL_PAD = 8
POOL_BASE = POOL_PAD + POOL_CARRY
D_FOX = 512
FOX_HEADS = 8
FOX_HEAD_DIM = 64
FOX_PAIRS = FOX_HEADS // 2
N_MEM = 256
MEM_HEADS = 4
MEM_HEAD_DIM = 256
D_FF = 2816
CONV_W = 3
CONV_PAD = 8
EPS = 1e-6
LOG2E = math.log2(math.e)
LANES = 128
VMEM_LIMIT = 60 * 1024 * 1024

F32 = jnp.float32
BF16 = jnp.bfloat16


def _dot(a, b):
    return jnp.dot(a, b, preferred_element_type=F32)


def _dot_nt(a, b):
    return lax.dot_general(a, b, (((1,), (1,)), ((), ())), preferred_element_type=F32)


def _rms(x, g):
    ms = jnp.mean(x * x, axis=-1, keepdims=True)
    return (x * lax.rsqrt(ms + EPS)) * g


def _const_spec(shape):
    nd = len(shape)
    return pl.BlockSpec(shape, lambda *_: (0,) * nd, pipeline_mode=pl.Buffered(1))


def _params(sem):
    return pltpu.CompilerParams(dimension_semantics=sem, vmem_limit_bytes=VMEM_LIMIT)


def _tril(n):
    r = lax.broadcasted_iota(jnp.int32, (n, n), 0)
    c = lax.broadcasted_iota(jnp.int32, (n, n), 1)
    return jnp.where(r >= c, 1.0, 0.0).astype(BF16)


def _triu(n):
    r = lax.broadcasted_iota(jnp.int32, (n, n), 0)
    c = lax.broadcasted_iota(jnp.int32, (n, n), 1)
    return jnp.where(r <= c, 1.0, 0.0).astype(BF16)


def _split3(x):
    hi = x.astype(BF16)
    r1 = x - hi.astype(F32)
    mid = r1.astype(BF16)
    lo = (r1 - mid.astype(F32)).astype(BF16)
    return hi, mid, lo


def _cumsum_rows(x, tril, carry):
    hi, mid, lo = _split3(x)
    return _dot(tril, hi) + _dot(tril, mid) + _dot(tril, lo) + carry


def _cumsum_lanes(x, triu, carry):
    hi, mid, lo = _split3(x)
    return _dot(hi, triu) + _dot(mid, triu) + _dot(lo, triu) + carry


def _pool_mixer(uext_ref, tmp_a, tmp_b, tq, t0, pos0, wpool_ref, pscale_ref, out_ref):
    lo, hi = POOL_PAD, POOL_BASE + tq
    t_idx = t0 + lax.broadcasted_iota(jnp.int32, (tq, 1), 0)
    for g, w in enumerate(POOL_WINDOWS):
        sl = slice(g * POOL_GROUP_W, (g + 1) * POOL_GROUP_W)
        src, bufs, d = uext_ref, (tmp_a, tmp_b), 1
        col = sl
        while 2 * d < w:
            dst = bufs[0]
            dst[lo:hi, :] = src[lo:hi, col] + src[lo - d:hi - d, col]
            src, bufs, d, col = dst, (bufs[1], bufs[0]), 2 * d, slice(0, POOL_GROUP_W)
        s = src[POOL_BASE:hi, col] + src[POOL_BASE - d:hi - d, col]
        u = uext_ref[POOL_BASE:hi, sl]
        count = jnp.minimum(pos0 + t_idx + 1, w).astype(F32)
        diff = (s / count - u).astype(BF16)
        y = _dot(diff, wpool_ref[g]) * pscale_ref[:, sl]
        out_ref[:, sl] = y.astype(out_ref.dtype)


def _fold_lanes(x, op):
    parts = [x[:, c:c + LANES] for c in range(0, x.shape[1], LANES)]
    return functools.reduce(op, parts)


def _inproj_kernel(x_ref, g_ref, w_ref, wf_ref, bf_ref, wpool_ref, pscale_ref, *refs, tm, sub, bps, pool):
    if pool:
        op_ref, ut_ref, q_ref, k_ref, v_ref, lf_ref, kb_ref, vb_ref, uext, tmp_a, tmp_b = refs
    else:
        u_ref, q_ref, k_ref, v_ref, lf_ref = refs
    if pool:
        blk = pl.program_id(0) % bps

        @pl.when(blk == 0)
        def _no_history():
            uext[0:POOL_BASE, :] = jnp.zeros((POOL_BASE, D_POOL), F32)
            tmp_a[0:POOL_PAD, :] = jnp.zeros((POOL_PAD, POOL_GROUP_W), F32)
            tmp_b[0:POOL_PAD, :] = jnp.zeros((POOL_PAD, POOL_GROUP_W), F32)

    subs = [slice(r0, r0 + sub) for r0 in range(0, tm, sub)]
    hs = [_rms(x_ref[rs, :], g_ref[...]).astype(BF16) for rs in subs]
    for rs, h in zip(subs, hs):
        u = _dot(h, w_ref[:, 0:D_POOL])
        if pool:
            uext[POOL_BASE + rs.start:POOL_BASE + rs.stop, :] = u
        else:
            u_ref[rs, :] = u
    if pool:
        _pool_mixer(uext, tmp_a, tmp_b, tm, blk * tm, 0, wpool_ref, pscale_ref, op_ref)
        tail = uext[tm + POOL_BASE - POOL_CARRY:tm + POOL_BASE, :]
        ut_ref[0] = tail
        uext[POOL_BASE - POOL_CARRY:POOL_BASE, :] = tail
    for rs, h in zip(subs, hs):
        qkv = _dot(h, w_ref[:, D_POOL:D_POOL + 3 * D_FOX])
        q_ref[rs, :] = (qkv[:, 0:D_FOX] * (FOX_HEAD_DIM ** -0.5 * LOG2E)).astype(BF16)
        k = qkv[:, D_FOX:2 * D_FOX]
        v = qkv[:, 2 * D_FOX:3 * D_FOX]
        zf = _dot(h, wf_ref[...]) + bf_ref[...]
        lf = -(jnp.maximum(-zf, 0.0) + jnp.log1p(jnp.exp(-jnp.abs(zf))))
        if pool:
            k_t = k.T.reshape(FOX_HEADS, FOX_HEAD_DIM, sub)
            v_t = v.T.reshape(FOX_HEADS, FOX_HEAD_DIM, sub)
            k_ref[0, :, :, rs] = k_t
            v_ref[0, :, :, rs] = v_t
            kb_ref[0, :, :, rs] = k_t.astype(BF16)
            vb_ref[0, :, :, rs] = v_t.astype(BF16)
            lf_ref[0, :, rs] = lf.T[0:FOX_HEADS, :]
        else:
            k_ref[rs, :] = k
            v_ref[rs, :] = v
            lf_ref[rs, :] = lf[:, 0:FOX_HEADS]


def _inproj(x2d, g, w_main, w_f, b_f, w_pool, pool_scale, tm, seq_len, pool):
    n = x2d.shape[0]
    bps = seq_len // tm
    row = lambda i: (i, 0)
    if pool:
        nb = n // seq_len
        kv_spec = pl.BlockSpec((1, FOX_HEADS, FOX_HEAD_DIM, tm), lambda i: (i // bps, 0, 0, i % bps))
        kv_shape = jax.ShapeDtypeStruct((nb, FOX_HEADS, FOX_HEAD_DIM, seq_len), F32)
        kvb_shape = jax.ShapeDtypeStruct((nb, FOX_HEADS, FOX_HEAD_DIM, seq_len), BF16)
        out_specs = [pl.BlockSpec((tm, D_POOL), row),
                     pl.BlockSpec((1, POOL_CARRY, D_POOL), lambda i: (i // bps, 0, 0)),
                     pl.BlockSpec((tm, D_FOX), row), kv_spec, kv_spec,
                     pl.BlockSpec((1, FOX_HEADS, tm), lambda i: (i // bps, 0, i % bps)),
                     kv_spec, kv_spec]
        out_shape = [jax.ShapeDtypeStruct((n, D_POOL), BF16),
                     jax.ShapeDtypeStruct((nb, POOL_CARRY, D_POOL), F32),
                     jax.ShapeDtypeStruct((n, D_FOX), BF16), kv_shape, kv_shape,
                     jax.ShapeDtypeStruct((nb, FOX_HEADS, seq_len), F32),
                     kvb_shape, kvb_shape]
        scratch = [pltpu.VMEM((POOL_BASE + tm, D_POOL), F32),
                   pltpu.VMEM((POOL_BASE + tm, POOL_GROUP_W), F32),
                   pltpu.VMEM((POOL_BASE + tm, POOL_GROUP_W), F32)]
    else:
        out_specs = [pl.BlockSpec((tm, D_POOL), row), pl.BlockSpec((tm, D_FOX), row),
                     pl.BlockSpec((tm, D_FOX), row), pl.BlockSpec((tm, D_FOX), row),
                     pl.BlockSpec((tm, FOX_HEADS), row)]
        out_shape = [jax.ShapeDtypeStruct((n, D_POOL), F32), jax.ShapeDtypeStruct((n, D_FOX), BF16),
                     jax.ShapeDtypeStruct((n, D_FOX), F32), jax.ShapeDtypeStruct((n, D_FOX), F32),
                     jax.ShapeDtypeStruct((n, FOX_HEADS), F32)]
        scratch = []
    return pl.pallas_call(
        functools.partial(_inproj_kernel, tm=tm, sub=min(tm, 512), bps=bps, pool=pool),
        grid=(n // tm,),
        in_specs=[
            pl.BlockSpec((tm, D_MODEL), row),
            _const_spec((1, D_MODEL)),
            _const_spec((D_MODEL, D_POOL + 3 * D_FOX)),
            _const_spec((D_MODEL, LANES)),
            _const_spec((1, LANES)),
            _const_spec((len(POOL_WINDOWS), POOL_GROUP_W, POOL_GROUP_W)),
            _const_spec((1, D_POOL)),
        ],
        out_specs=out_specs,
        out_shape=out_shape,
        scratch_shapes=scratch,
        compiler_params=_params(("arbitrary",)),
        name="inproj_pool" if pool else "inproj",
    )(x2d, g, w_main, w_f, b_f, w_pool, pool_scale)


def _memkv_kernel(m_ref, g_ref, wk_ref, wv_ref, mk_ref, mv_ref, mk4_ref, mv4_ref):
    h = _rms(m_ref[...], g_ref[...]).astype(BF16)
    mk = _dot(h, wk_ref[...])
    mv = _dot(h, wv_ref[...])
    mk_ref[...] = mk
    mv_ref[...] = mv
    for hd in range(MEM_HEADS):
        hs = slice(hd * MEM_HEAD_DIM, (hd + 1) * MEM_HEAD_DIM)
        mk4_ref[:, hd, :] = mk[:, hs]
        mv4_ref[:, hd, :] = mv[:, hs]


def _memkv(m2d, g, wk, wv, tm):
    n = m2d.shape[0]
    row = lambda i: (i, 0)
    row3 = lambda i: (i, 0, 0)
    return pl.pallas_call(
        _memkv_kernel,
        grid=(n // tm,),
        in_specs=[
            pl.BlockSpec((tm, D_MODEL), row),
            _const_spec((1, D_MODEL)),
            _const_spec((D_MODEL, D_MODEL)),
            _const_spec((D_MODEL, D_MODEL)),
        ],
        out_specs=[pl.BlockSpec((tm, D_MODEL), row), pl.BlockSpec((tm, D_MODEL), row),
                   pl.BlockSpec((tm, MEM_HEADS, MEM_HEAD_DIM), row3),
                   pl.BlockSpec((tm, MEM_HEADS, MEM_HEAD_DIM), row3)],
        out_shape=[jax.ShapeDtypeStruct((n, D_MODEL), F32)] * 2
        + [jax.ShapeDtypeStruct((n, MEM_HEADS, MEM_HEAD_DIM), F32)] * 2,
        compiler_params=_params(("arbitrary",)),
        name="memkv",
    )(m2d, g, wk, wv)


def _fox_prompt_kernel(q_ref, k_ref, v_ref, lf_ref, o_ref, ccol, crow, cpad, *, nq, tq):
    pair = pl.program_id(1)
    lane = lax.broadcasted_iota(jnp.int32, (1, LANES), 1)
    low = lane < FOX_HEAD_DIM

    @pl.when(pair == 0)
    def _cumsum():
        triu = _triu(tq)
        cpad[...] = jnp.zeros_like(cpad)
        carry = jnp.zeros((FOX_HEADS, 1), F32)
        for b in range(nq):
            c = _cumsum_lanes(lf_ref[0, :, b * tq:(b + 1) * tq], triu, carry)
            carry = c[:, tq - 1:tq]
            c2 = c * LOG2E
            crow[b] = c2
            cpad[0:FOX_HEADS, :] = c2
            ccol[b] = cpad[...].T

    r_idx = lax.broadcasted_iota(jnp.int32, (tq, tq), 0)
    c_idx = lax.broadcasted_iota(jnp.int32, (tq, tq), 1)
    causal = r_idx >= c_idx
    zero = jnp.zeros((), BF16)
    for i in reversed(range(nq)):
        keys = (i + 1) * tq
        qp = q_ref[0, i * tq:(i + 1) * tq, :]
        cblk = ccol[i]
        o_blk = None
        for odd in range(2):
            head = 2 * pair + odd
            q_h = jnp.where(low, zero, qp) if odd else jnp.where(low, qp, zero)
            v_own = v_ref[0, odd, :, 0:keys]
            v_pad = jnp.zeros_like(v_own)
            v_h = jnp.concatenate([v_pad, v_own] if odd else [v_own, v_pad], axis=0)
            cq = jnp.sum(jnp.where(lane == head, cblk, 0.0), axis=-1, keepdims=True)
            s_list = []
            for j in range(i + 1):
                k_j = k_ref[0, :, :, j * tq:(j + 1) * tq].reshape(LANES, tq)
                s = _dot(q_h, k_j) - crow[j, pl.ds(head, 1), :]
                if j == i:
                    s = jnp.where(causal, s, -jnp.inf)
                s_list.append(s)
            m = functools.reduce(jnp.maximum, [_fold_lanes(s, jnp.maximum) for s in s_list])
            m_row = jnp.max(m, axis=-1, keepdims=True) + cq
            r = cq - m_row
            p_list = [jnp.exp2(s + r) for s in s_list]
            l = functools.reduce(jnp.add, [_fold_lanes(p, jnp.add) for p in p_list])
            l = jnp.sum(l, axis=-1, keepdims=True)
            p_cat = jnp.concatenate([p.astype(BF16) for p in p_list], axis=1)
            o_h = _dot_nt(p_cat, v_h) / l
            o_blk = o_h if o_blk is None else o_blk + o_h
        o_ref[0, i * tq:(i + 1) * tq, :] = o_blk.astype(BF16)


def _fox_prompt(q, k_t, v_t, lf_t, tq):
    bsz, t, _ = q.shape
    nq = t // tq
    col = lambda b, p: (b, 0, p)
    kv_spec = pl.BlockSpec((1, 2, FOX_HEAD_DIM, t), lambda b, p: (b, p, 0, 0))
    return pl.pallas_call(
        functools.partial(_fox_prompt_kernel, nq=nq, tq=tq),
        grid=(bsz, FOX_PAIRS),
        in_specs=[
            pl.BlockSpec((1, t, LANES), col),
            kv_spec,
            kv_spec,
            pl.BlockSpec((1, FOX_HEADS, t), lambda b, p: (b, 0, 0)),
        ],
        out_specs=pl.BlockSpec((1, t, LANES), col),
        out_shape=jax.ShapeDtypeStruct((bsz, t, D_FOX), BF16),
        scratch_shapes=[
            pltpu.VMEM((nq, tq, LANES), F32),
            pltpu.VMEM((nq, FOX_HEADS, tq), F32),
            pltpu.VMEM((LANES, tq), F32),
        ],
        compiler_params=_params(("arbitrary", "arbitrary")),
        name="fox_prompt",
    )(q, k_t, v_t, lf_t)


def _mix_sample_kernel(u_ref, q_ref, k_ref, v_ref, lf_ref, kc_ref, vc_ref, lfc_ref, ph_ref,
                       wpool_ref, pscale_ref, op_ref, of_ref, cbuf, crow, uext, tmp_a, tmp_b,
                       *, past, t, tc):
    nblk = past // tc
    triu = _triu(tc)
    carry = jnp.zeros((FOX_HEADS, 1), F32)
    for b in range(nblk):
        c = _cumsum_lanes(lfc_ref[0, :, b * tc:(b + 1) * tc], triu, carry)
        carry = c[:, tc - 1:tc]
        crow[:, b * tc:(b + 1) * tc] = c * LOG2E
    cbuf[...] = jnp.zeros_like(cbuf)
    cbuf[0:t, 0:FOX_HEADS] = lf_ref[0]
    cnew = _cumsum_rows(cbuf[...], _tril(LANES), jnp.zeros((1, LANES), F32))
    cnew_row = (cnew.T[0:FOX_HEADS, 0:t] + carry) * LOG2E

    uext[0:POOL_BASE - POOL_HIST, :] = jnp.zeros((POOL_BASE - POOL_HIST, D_POOL), F32)
    tmp_a[0:POOL_PAD, :] = jnp.zeros((POOL_PAD, POOL_GROUP_W), F32)
    tmp_b[0:POOL_PAD, :] = jnp.zeros((POOL_PAD, POOL_GROUP_W), F32)
    uext[POOL_BASE - POOL_HIST:POOL_BASE, :] = ph_ref[0]
    uext[POOL_BASE:POOL_BASE + t, :] = u_ref[0]
    _pool_mixer(uext, tmp_a, tmp_b, t, 0, past, wpool_ref, pscale_ref, op_ref.at[0])

    r_idx = lax.broadcasted_iota(jnp.int32, (t, t), 0)
    c_idx = lax.broadcasted_iota(jnp.int32, (t, t), 1)
    causal = r_idx >= c_idx
    for h in range(FOX_HEADS):
        hs = slice(h * FOX_HEAD_DIM, (h + 1) * FOX_HEAD_DIM)
        q_h = q_ref[0, :, hs]
        cq = (cnew[0:t, h:h + 1] + carry[h:h + 1, :]) * LOG2E
        s_hist = _dot(q_h, kc_ref[0, h].astype(BF16)) - crow[h:h + 1, :]
        s_new = _dot_nt(q_h, k_ref[0, :, hs].astype(BF16)) - cnew_row[h:h + 1, :]
        s_new = jnp.where(causal, s_new, -jnp.inf)
        m_row = jnp.maximum(jnp.max(s_hist, axis=-1, keepdims=True),
                            jnp.max(s_new, axis=-1, keepdims=True)) + cq
        r = cq - m_row
        p_hist = jnp.exp2(s_hist + r)
        p_new = jnp.exp2(s_new + r)
        l = jnp.sum(p_hist, axis=-1, keepdims=True) + jnp.sum(p_new, axis=-1, keepdims=True)
        acc = (_dot_nt(p_hist.astype(BF16), vc_ref[0, h].astype(BF16))
               + _dot(p_new.astype(BF16), v_ref[0, :, hs].astype(BF16)))
        of_ref[0, :, hs] = (acc / l).astype(BF16)


def _mix_sample(u, q, k, v, lf, kc_t, vc_t, lfc_t, ph, w_pool, pool_scale, tc):
    bsz, t, _ = u.shape
    past = kc_t.shape[3]
    assert t <= LANES and past % tc == 0 and t >= POOL_HIST
    one = lambda b: (b, 0, 0)
    cache_spec = pl.BlockSpec((1, FOX_HEADS, FOX_HEAD_DIM, past), lambda b: (b, 0, 0, 0))
    kern = functools.partial(_mix_sample_kernel, past=past, t=t, tc=tc)
    return pl.pallas_call(
        kern,
        grid=(bsz,),
        in_specs=[
            pl.BlockSpec((1, t, D_POOL), one),
            pl.BlockSpec((1, t, D_FOX), one),
            pl.BlockSpec((1, t, D_FOX), one),
            pl.BlockSpec((1, t, D_FOX), one),
            pl.BlockSpec((1, t, FOX_HEADS), one),
            cache_spec,
            cache_spec,
            pl.BlockSpec((1, FOX_HEADS, past), one),
            pl.BlockSpec((1, POOL_HIST, D_POOL), one),
            _const_spec((len(POOL_WINDOWS), POOL_GROUP_W, POOL_GROUP_W)),
            _const_spec((1, D_POOL)),
        ],
        out_specs=[pl.BlockSpec((1, t, D_POOL), one), pl.BlockSpec((1, t, D_FOX), one)],
        out_shape=[jax.ShapeDtypeStruct((bsz, t, D_POOL), BF16),
                   jax.ShapeDtypeStruct((bsz, t, D_FOX), BF16)],
        scratch_shapes=[
            pltpu.VMEM((LANES, LANES), F32),
            pltpu.VMEM((FOX_HEADS, past), F32),
            pltpu.VMEM((POOL_BASE + t, D_POOL), F32),
            pltpu.VMEM((POOL_BASE + t, POOL_GROUP_W), F32),
            pltpu.VMEM((POOL_BASE + t, POOL_GROUP_W), F32),
        ],
        compiler_params=_params(("arbitrary",)),
        name="mix_sample",
    )(u, q, k, v, lf, kc_t, vc_t, lfc_t, ph, w_pool, pool_scale)


def _row_groups(nb, tm, sub):
    if tm >= sub:
        return [(b, b + 1, r0, r0 + sub) for b in range(nb) for r0 in range(0, tm, sub)]
    per = sub // tm
    return [(b0, b0 + per, 0, tm) for b0 in range(0, nb, per)]


def _memffn_kernel(x_ref, op_ref, of_ref, mk_ref, mv_ref, ch_ref, wo_ref, gxq_ref, wmq_ref, wmo_ref,
                   gffn_ref, wup_ref, cw_ref, cb_ref, wdown_ref, gfin_ref, y_ref, cs_ref, aext, mo_sc,
                   *, nb, tm, sub, final_norm):
    @pl.when(pl.program_id(1) == 0)
    def _hist():
        for b in range(nb):
            aext[b, CONV_PAD - (CONV_W - 1):CONV_PAD, :] = ch_ref[b]

    groups = _row_groups(nb, tm, sub)

    def load(ref, g):
        b0, b1, r0, r1 = g
        return ref[b0:b1, r0:r1, :].reshape(sub, ref.shape[-1])

    def pieces(g):
        b0, b1, r0, r1 = g
        n = r1 - r0
        return [(b, slice((b - b0) * n, (b - b0 + 1) * n), slice(r0, r1)) for b in range(b0, b1)]

    xs = [load(x_ref, g) + _dot(jnp.concatenate([load(op_ref, g), load(of_ref, g)], axis=1), wo_ref[...])
          for g in groups]

    mqs = [(_dot(_rms(x, gxq_ref[...]).astype(BF16), wmq_ref[...]) * (MEM_HEAD_DIM ** -0.5)).astype(BF16)
           for x in xs]
    for b in range(nb):
        for h in range(MEM_HEADS):
            hs = slice(h * MEM_HEAD_DIM, (h + 1) * MEM_HEAD_DIM)
            mk_h = mk_ref[b, :, hs].astype(BF16)
            mv_h = mv_ref[b, :, hs].astype(BF16)
            for gi, g in enumerate(groups):
                for pb, pr, _ in pieces(g):
                    if pb != b:
                        continue
                    s = _dot_nt(mqs[gi][pr, hs], mk_h)
                    m = jnp.max(s, axis=-1, keepdims=True)
                    p = jnp.exp(s - m)
                    l = jnp.sum(p, axis=-1, keepdims=True)
                    o = _dot(p.astype(BF16), mv_h) / l
                    mo_sc[gi * sub + pr.start:gi * sub + pr.stop, hs] = o.astype(BF16)
    x2s = [x + _dot(mo_sc[gi * sub:(gi + 1) * sub, :], wmo_ref[...]) for gi, x in enumerate(xs)]

    hfs = [_rms(x2, gffn_ref[...]).astype(BF16) for x2 in x2s]
    downs = []
    for g, hf in zip(groups, hfs):
        a = _dot(hf, wup_ref[:, 0:D_FF])
        b_up = _dot(hf, wup_ref[:, D_FF:2 * D_FF])
        convs = []
        for pb, pr, sr in pieces(g):
            r0, r1 = CONV_PAD + sr.start, CONV_PAD + sr.stop
            aext[pb, r0:r1, :] = a[pr, :]
            convs.append(aext[pb, r0 - 2:r1 - 2, :] * cw_ref[0:1, :]
                         + aext[pb, r0 - 1:r1 - 1, :] * cw_ref[1:2, :]
                         + a[pr, :] * cw_ref[2:3, :])
        conv = convs[0] if len(convs) == 1 else jnp.concatenate(convs, axis=0)
        conv = cb_ref[...] + conv
        gate = conv * (1.0 / (1.0 + jnp.exp(-conv)))
        downs.append(_dot((gate * b_up).astype(BF16), wdown_ref[...]))
    for b in range(nb):
        cs_ref[b] = aext[b, CONV_PAD + tm - (CONV_W - 1):CONV_PAD + tm, :]
        aext[b, 0:CONV_PAD, :] = aext[b, tm:tm + CONV_PAD, :]
    for g, x2, down in zip(groups, x2s, downs):
        b0, b1, r0, r1 = g
        x3 = x2 + down
        if final_norm:
            x3 = _rms(x3, gfin_ref[...])
        y_ref[b0:b1, r0:r1, :] = x3.reshape(b1 - b0, r1 - r0, D_MODEL)


def _memffn(x, o_pool, o_fox, mk, mv, conv_hist, w_o, g_xq, w_mq, w_mo, g_ffn, w_up, conv_w, conv_b,
            w_down, g_final, nb, tm, final_norm):
    bsz, t, _ = x.shape
    sub = min(nb * tm, 512)
    assert bsz % nb == 0 and t % tm == 0 and (nb * tm) % sub == 0 and (tm % sub == 0 or sub % tm == 0)
    blk = lambda b, i: (b, i, 0)
    full = lambda b, i: (b, 0, 0)
    kern = functools.partial(_memffn_kernel, nb=nb, tm=tm, sub=sub, final_norm=final_norm)
    return pl.pallas_call(
        kern,
        grid=(bsz // nb, t // tm),
        in_specs=[
            pl.BlockSpec((nb, tm, D_MODEL), blk),
            pl.BlockSpec((nb, tm, D_POOL), blk),
            pl.BlockSpec((nb, tm, D_FOX), blk),
            pl.BlockSpec((nb, N_MEM, D_MODEL), full),
            pl.BlockSpec((nb, N_MEM, D_MODEL), full),
            pl.BlockSpec((nb, CONV_W - 1, D_FF), full),
            _const_spec((D_MODEL, D_MODEL)),
            _const_spec((1, D_MODEL)),
            _const_spec((D_MODEL, D_MODEL)),
            _const_spec((D_MODEL, D_MODEL)),
            _const_spec((1, D_MODEL)),
            _const_spec((D_MODEL, 2 * D_FF)),
            _const_spec((CONV_W, D_FF)),
            _const_spec((1, D_FF)),
            _const_spec((D_FF, D_MODEL)),
            _const_spec((1, D_MODEL)),
        ],
        out_specs=[
            pl.BlockSpec((nb, tm, D_MODEL), blk),
            pl.BlockSpec((nb, CONV_W - 1, D_FF), full),
        ],
        out_shape=[
            jax.ShapeDtypeStruct((bsz, t, D_MODEL), F32),
            jax.ShapeDtypeStruct((bsz, CONV_W - 1, D_FF), F32),
        ],
        scratch_shapes=[
            pltpu.VMEM((nb, CONV_PAD + tm, D_FF), F32),
            pltpu.VMEM((nb * tm, D_MODEL), BF16),
        ],
        compiler_params=_params(("arbitrary", "arbitrary")),
        name="memffn",
    )(x, o_pool, o_fox, mk, mv, conv_hist, w_o, g_xq, w_mq, w_mo, g_ffn, w_up, conv_w, conv_b, w_down,
      g_final)


def _row_tile(n, target):
    tm = min(n, target)
    assert n % tm == 0
    return tm


def kernel(x_prompt, x_sample, cache_fox_k, cache_fox_v, cache_fox_logf, state_pool, state_ffn_conv, cache_mem_k, cache_mem_v, mem_prompt, g_mix, w_in, b_f, w_pool, pool_scale, w_o, g_xq, g_mkv, w_mq, w_mk, w_mv, w_mo, g_ffn, w_up, conv_w, conv_b, w_down, g_final):
    depth = w_in.shape[0]
    bp, tp, _ = x_prompt.shape
    bs, ts, _ = x_sample.shape
    past = cache_fox_k.shape[2]
    n_main = D_POOL + 3 * D_FOX

    xp, xs = x_prompt, x_sample
    outs = {name: [] for name in ("pk", "pv", "plf", "ppool", "pconv", "pmk", "pmv",
                                   "sk", "sv", "slf", "spool", "sconv")}
    for l in range(depth):
        last = l == depth - 1
        row2 = lambda a: a.reshape(1, -1).astype(F32)
        w_main = w_in[l][:, :n_main].astype(BF16)
        w_f = jnp.pad(w_in[l][:, n_main:], ((0, 0), (0, LANES - FOX_HEADS))).astype(BF16)
        bf = jnp.pad(b_f[l].astype(F32), (0, LANES - FOX_HEADS)).reshape(1, LANES)
        wp = w_pool[l].astype(BF16)
        ps = row2(pool_scale[l])
        ffn_w = (w_o[l].astype(BF16), row2(g_xq[l]), w_mq[l].astype(BF16), w_mo[l].astype(BF16),
                 row2(g_ffn[l]), w_up[l].astype(BF16), conv_w[l].astype(F32), row2(conv_b[l]),
                 w_down[l].astype(BF16), row2(g_final))

        mk, mv, mk4, mv4 = _memkv(mem_prompt.reshape(bp * N_MEM, D_MODEL), row2(g_mkv[l]),
                                  w_mk[l].astype(BF16), w_mv[l].astype(BF16), _row_tile(bp * N_MEM, 512))
        mk = mk.reshape(bp, N_MEM, D_MODEL)
        mv = mv.reshape(bp, N_MEM, D_MODEL)
        o_pool, u_tail, q, k_t, v_t, lf_t, kb_t, vb_t = _inproj(
            xp.reshape(bp * tp, D_MODEL), row2(g_mix[l]), w_main, w_f, bf, wp, ps, _row_tile(tp, 1024), tp,
            pool=True)
        o_fox = _fox_prompt(q.reshape(bp, tp, D_FOX), kb_t, vb_t, lf_t, _row_tile(tp, 256))
        xp, pconv = _memffn(xp, o_pool.reshape(bp, tp, D_POOL), o_fox, mk, mv,
                            jnp.zeros((bp, CONV_W - 1, D_FF), F32), *ffn_w,
                            nb=1, tm=_row_tile(tp, 512), final_norm=last)
        outs["pk"].append(jnp.transpose(k_t, (0, 3, 1, 2)))
        outs["pv"].append(jnp.transpose(v_t, (0, 3, 1, 2)))
        outs["plf"].append(jnp.transpose(lf_t, (0, 2, 1)))
        outs["ppool"].append(u_tail[:, POOL_CARRY - POOL_HIST:, :])
        outs["pconv"].append(pconv)
        outs["pmk"].append(mk4.reshape(bp, N_MEM, MEM_HEADS, MEM_HEAD_DIM))
        outs["pmv"].append(mv4.reshape(bp, N_MEM, MEM_HEADS, MEM_HEAD_DIM))

        u, q, k, v, lf = _inproj(xs.reshape(bs * ts, D_MODEL), row2(g_mix[l]), w_main, w_f, bf,
                                 wp, ps, _row_tile(bs * ts, 512), ts, pool=False)
        u = u.reshape(bs, ts, D_POOL)
        k = k.reshape(bs, ts, D_FOX)
        v = v.reshape(bs, ts, D_FOX)
        lf = lf.reshape(bs, ts, FOX_HEADS)
        o_pool, o_fox = _mix_sample(u, q.reshape(bs, ts, D_FOX), k, v, lf,
                                    jnp.transpose(cache_fox_k[l], (0, 2, 3, 1)),
                                    jnp.transpose(cache_fox_v[l], (0, 2, 3, 1)),
                                    jnp.transpose(cache_fox_logf[l].astype(F32), (0, 2, 1)),
                                    state_pool[l], wp, ps, tc=512)
        xs, sconv = _memffn(xs, o_pool, o_fox, cache_mem_k[l].reshape(bs, N_MEM, D_MODEL),
                            cache_mem_v[l].reshape(bs, N_MEM, D_MODEL), state_ffn_conv[l], *ffn_w,
                            nb=_row_tile(bs, 4), tm=ts, final_norm=last)
        outs["sk"].append(k.reshape(bs, ts, FOX_HEADS, FOX_HEAD_DIM))
        outs["sv"].append(v.reshape(bs, ts, FOX_HEADS, FOX_HEAD_DIM))
        outs["slf"].append(lf)
        outs["spool"].append(u[:, ts - POOL_HIST:, :])
        outs["sconv"].append(sconv)

    st = {name: jnp.stack(vals) for name, vals in outs.items()}
    return (xp, xs, st["pk"], st["pv"], st["plf"], st["ppool"], st["pconv"], st["pmk"], st["pmv"],
            st["sk"], st["sv"], st["slf"], st["spool"], st["sconv"])
```
